```python
import jax
import jax.numpy as jnp
from jax import lax
import numpy as np

D_MODEL = 2048
BATCH = 1
SEQ = 16384
DEPTH = 4
DEC_BATCH = 16
DEC_SEQ = 64
PAST_LEN = 1024

CHUNK = 64
N_META = 16
Q_BLOCK = 128
N_MIXERS = 2
N_FOX = (DEPTH + 1) // 2
N_RET = DEPTH // 2
DH_FOX = 128
H_FOX = D_MODEL // DH_FOX
DK_RET = 256
H_RET = D_MODEL // DK_RET
DV_RET = 2 * DK_RET
D_VRET = H_RET * DV_RET
D_FF = 4 * D_MODEL
FOX_IN = 4 * D_MODEL + H_FOX
RET_IN = 2 * H_RET * DK_RET + 2 * D_VRET
ROPE_BASE = 10000.0
EPS = 1e-6
NEG_INF = -1e30

kernel_name = "fox_retention_hybrid_stream_step"


def rmsnorm(x, g):
    xf = x.astype(jnp.float32)
    y = xf * lax.rsqrt(jnp.mean(xf * xf, axis=-1, keepdims=True) + EPS)
    return (y * g.astype(jnp.float32)).astype(x.dtype)


def sq_relu_mlp(x, w_up, w_down):
    h = jax.nn.relu(x @ w_up)
    return (h * h) @ w_down


def fox_project(h, w_in, g_q, g_k, b_f):
    B, T, _ = h.shape
    proj = h @ w_in
    q, k, v, gate, f_logit = jnp.split(proj, [D_MODEL, 2 * D_MODEL, 3 * D_MODEL, 4 * D_MODEL], axis=-1)
    q = rmsnorm(q.reshape(B, T, H_FOX, DH_FOX), g_q)
    k = rmsnorm(k.reshape(B, T, H_FOX, DH_FOX), g_k)
    v = v.reshape(B, T, H_FOX, DH_FOX)
    logf = jax.nn.log_sigmoid((f_logit + b_f).astype(jnp.float32))
    return q, k, v, gate, logf


def fox_attend(q, cq, pq, k, v, ck, pk):
    s = jnp.einsum('bqhd,bkhd->bhqk', q, k, preferred_element_type=jnp.float32) * (DH_FOX ** -0.5)
    s = s + (cq.transpose(0, 2, 1)[..., :, None] - ck.transpose(0, 2, 1)[..., None, :])
    s = jnp.where(pk[None, :] <= pq[:, None], s, NEG_INF)
    p = jax.nn.softmax(s, axis=-1)
    return jnp.einsum('bhqk,bkhd->bqhd', p.astype(v.dtype), v)


def fox_output(o, gate, w_o):
    B, T = o.shape[:2]
    return (o.reshape(B, T, D_MODEL) * jax.nn.sigmoid(gate)) @ w_o


def fox_prompt(h, w_in, g_q, g_k, b_f, w_o):
    B, L, _ = h.shape
    q, k, v, gate, logf = fox_project(h, w_in, g_q, g_k, b_f)
    c = jnp.cumsum(logf, axis=1)
    pos = jnp.arange(L)
    o_meta = fox_attend(q[:, :N_META], c[:, :N_META], pos[:N_META], k, v, c, pos)
    n_blocks = (L - N_META) // Q_BLOCK

    def block(i):
        start = N_META + i * Q_BLOCK
        qb = lax.dynamic_slice_in_dim(q, start, Q_BLOCK, axis=1)
        cb = lax.dynamic_slice_in_dim(c, start, Q_BLOCK, axis=1)
        return fox_attend(qb, cb, start + jnp.arange(Q_BLOCK), k, v, c, pos)

    o_real = lax.map(block, jnp.arange(n_blocks))
    o_real = o_real.transpose(1, 0, 2, 3, 4).reshape(B, L - N_META, H_FOX, DH_FOX)
    o = jnp.concatenate([o_meta, o_real], axis=1)
    return fox_output(o, gate, w_o), k, v, logf


def fox_sample(h, k_cache, v_cache, logf_cache, w_in, g_q, g_k, b_f, w_o):
    B, T, _ = h.shape
    P = k_cache.shape[1]
    q, k, v, gate, logf = fox_project(h, w_in, g_q, g_k, b_f)
    k_all = jnp.concatenate([k_cache.astype(k.dtype), k], axis=1)
    v_all = jnp.concatenate([v_cache.astype(v.dtype), v], axis=1)
    c = jnp.cumsum(jnp.concatenate([logf_cache.astype(jnp.float32), logf], axis=1), axis=1)
    pos = jnp.arange(P + T)
    o = fox_attend(q, c[:, P:], pos[P:], k_all, v_all, c, pos)
    return fox_output(o, gate, w_o), k, v, logf


def ret_log_gamma():
    return jnp.log(1.0 - jnp.power(2.0, -5.0 - jnp.arange(H_RET, dtype=jnp.float32)))


def rotary(x, pos):
    half = x.shape[-1] // 2
    inv = jnp.power(ROPE_BASE, -jnp.arange(half, dtype=jnp.float32) / half)
    ang = pos.astype(jnp.float32)[:, None] * inv[None, :]
    cos = jnp.cos(ang)[None, :, None, :]
    sin = jnp.sin(ang)[None, :, None, :]
    xf = x.astype(jnp.float32)
    x1, x2 = xf[..., :half], xf[..., half:]
    return jnp.concatenate([x1 * cos - x2 * sin, x1 * sin + x2 * cos], axis=-1)


def retention_project(h, w_in, pos):
    B, T, _ = h.shape
    proj = h @ w_in
    dqk = H_RET * DK_RET
    q, k, v, g = jnp.split(proj, [dqk, 2 * dqk, 2 * dqk + D_VRET], axis=-1)
    q = rotary(q.reshape(B, T, H_RET, DK_RET), pos)
    k = rotary(k.reshape(B, T, H_RET, DK_RET), pos) * (DK_RET ** -0.5)
    v = v.reshape(B, T, H_RET, DV_RET).astype(jnp.float32)
    return q, k, v, g


def retention_chunk(S, q, k, v, log_gamma):
    T = q.shape[1]
    idx = jnp.arange(T, dtype=jnp.float32)
    diff = idx[:, None] - idx[None, :]
    decay = jnp.where(diff >= 0, jnp.exp(jnp.maximum(diff, 0.0)[None] * log_gamma[:, None, None]), 0.0)
    scores = jnp.einsum('bthd,bshd->bhts', q, k) * decay[None]
    inner = jnp.einsum('bhts,bshv->bthv', scores, v)
    q_decay = jnp.exp((idx + 1.0)[None, :] * log_gamma[:, None])
    cross = jnp.einsum('bthd,bhdv->bthv', q, S) * q_decay.T[None, :, :, None]
    k_decay = jnp.exp((T - 1.0 - idx)[None, :] * log_gamma[:, None])
    S_new = jnp.exp(T * log_gamma)[None, :, None, None] * S + jnp.einsum('bthd,bthv->bhdv', k * k_decay.T[None, :, :, None], v)
    return S_new, inner + cross


def retention_output(o, g, g_norm, w_o):
    B, T = o.shape[:2]
    o = o * lax.rsqrt(jnp.mean(o * o, axis=-1, keepdims=True) + EPS) * g_norm.astype(jnp.float32)
    o = o.reshape(B, T, D_VRET).astype(g.dtype)
    return (jax.nn.silu(g) * o) @ w_o


def retention_prompt(h, w_in, g_norm, w_o):
    B, L, _ = h.shape
    q, k, v, g = retention_project(h, w_in, jnp.arange(L))
    pad = (-L) % CHUNK
    n_chunks = (L + pad) // CHUNK

    def blocks(t):
        t = jnp.pad(t, ((0, 0), (pad, 0), (0, 0), (0, 0)))
        return t.reshape(B, n_chunks, CHUNK, *t.shape[2:]).swapaxes(0, 1)

    log_gamma = ret_log_gamma()

    def step(S, qkv):
        qb, kb, vb = qkv
        return retention_chunk(S, qb, kb, vb, log_gamma)

    S0 = jnp.zeros((B, H_RET, DK_RET, DV_RET), jnp.float32)
    S_fin, o = lax.scan(step, S0, (blocks(q), blocks(k), blocks(v)))
    o = o.swapaxes(0, 1).reshape(B, L + pad, H_RET, DV_RET)[:, pad:]
    return retention_output(o, g, g_norm, w_o), S_fin


def retention_sample(h, S, w_in, g_norm, w_o):
    B, T, _ = h.shape
    pos = N_META + PAST_LEN + jnp.arange(T)
    q, k, v, g = retention_project(h, w_in, pos)
    S_new, o = retention_chunk(S.astype(jnp.float32), q, k, v, ret_log_gamma())
    return retention_output(o, g, g_norm, w_o), S_new


def setup_inputs(seed: int = 0) -> dict:
    key = jax.random.key(seed)
    ks = jax.random.split(key, 20)

    def nrm(k, shape, scale):
        return scale * jax.random.normal(k, shape, jnp.float32)

    return {
        "x_prompt": nrm(ks[0], (BATCH, SEQ, D_MODEL), 1.0),
        "x_sample": nrm(ks[1], (DEC_BATCH, DEC_SEQ, D_MODEL), 1.0),
        "cache_fox_k": nrm(ks[2], (N_FOX, DEC_BATCH, PAST_LEN, H_FOX, DH_FOX), 1.0),
        "cache_fox_v": nrm(ks[3], (N_FOX, DEC_BATCH, PAST_LEN, H_FOX, DH_FOX), 1.0),
        "cache_fox_logf": jax.nn.log_sigmoid(2.5 + jax.random.normal(ks[4], (N_FOX, DEC_BATCH, PAST_LEN, H_FOX), jnp.float32)),
        "state_ret": nrm(ks[5], (N_RET, DEC_BATCH, H_RET, DK_RET, DV_RET), 0.05),
        "meta_tokens": nrm(ks[6], (N_META, D_MODEL), 1.0),
        "norm_mix": 1.0 + nrm(ks[7], (DEPTH, D_MODEL), 0.05),
        "norm_mlp": 1.0 + nrm(ks[8], (DEPTH, D_MODEL), 0.05),
        "norm_final": 1.0 + nrm(ks[9], (D_MODEL,), 0.05),
        "fox_w_in": nrm(ks[10], (N_FOX, D_MODEL, FOX_IN), D_MODEL ** -0.5),
        "fox_g_q": 1.0 + nrm(ks[11], (N_FOX, DH_FOX), 0.05),
        "fox_g_k": 1.0 + nrm(ks[12], (N_FOX, DH_FOX), 0.05),
        "fox_b_f": jax.random.uniform(ks[13], (N_FOX, H_FOX), jnp.float32, 1.0, 4.0),
        "fox_w_o": nrm(ks[14], (N_FOX, D_MODEL, D_MODEL), D_MODEL ** -0.5),
        "ret_w_in": nrm(ks[15], (N_RET, D_MODEL, RET_IN), D_MODEL ** -0.5),
        "ret_g_norm": 1.0 + nrm(ks[16], (N_RET, H_RET, DV_RET), 0.05),
        "ret_w_o": nrm(ks[17], (N_RET, D_VRET, D_MODEL), D_VRET ** -0.5),
        "mlp_w_up": nrm(ks[18], (DEPTH, D_MODEL, D_FF), D_MODEL ** -0.5),
        "mlp_w_down": nrm(ks[19], (DEPTH, D_FF, D_MODEL), D_FF ** -0.5),
    }


def reference(x_prompt, x_sample, cache_fox_k, cache_fox_v, cache_fox_logf, state_ret, meta_tokens,
              norm_mix, norm_mlp, norm_final, fox_w_in, fox_g_q, fox_g_k, fox_b_f, fox_w_o,
              ret_w_in, ret_g_norm, ret_w_o, mlp_w_up, mlp_w_down):
    B = x_prompt.shape[0]
    meta = jnp.broadcast_to(meta_tokens[None].astype(x_prompt.dtype), (B, N_META, D_MODEL))
    hp = jnp.concatenate([meta, x_prompt], axis=1)
    hs = x_sample
    fk_p, fv_p, ff_p, rs_p = [], [], [], []
    fk_s, fv_s, ff_s, rs_s = [], [], [], []
    for i in range(DEPTH):
        j = i // N_MIXERS
        if i % N_MIXERS == 0:
            yp, kp, vp, lp = fox_prompt(rmsnorm(hp, norm_mix[i]), fox_w_in[j], fox_g_q[j], fox_g_k[j], fox_b_f[j], fox_w_o[j])
            ys, kn, vn, ln = fox_sample(rmsnorm(hs, norm_mix[i]), cache_fox_k[j], cache_fox_v[j], cache_fox_logf[j],
                                        fox_w_in[j], fox_g_q[j], fox_g_k[j], fox_b_f[j], fox_w_o[j])
            fk_p.append(kp); fv_p.append(vp); ff_p.append(lp)
            fk_s.append(kn); fv_s.append(vn); ff_s.append(ln)
        else:
            yp, Sp = retention_prompt(rmsnorm(hp, norm_mix[i]), ret_w_in[j], ret_g_norm[j], ret_w_o[j])
            ys, Sn = retention_sample(rmsnorm(hs, norm_mix[i]), state_ret[j], ret_w_in[j], ret_g_norm[j], ret_w_o[j])
            rs_p.append(Sp.astype(state_ret.dtype)); rs_s.append(Sn.astype(state_ret.dtype))
        hp = hp + yp
        hs = hs + ys
        hp = hp + sq_relu_mlp(rmsnorm(hp, norm_mlp[i]), mlp_w_up[i], mlp_w_down[i])
        hs = hs + sq_relu_mlp(rmsnorm(hs, norm_mlp[i]), mlp_w_up[i], mlp_w_down[i])
    y_prompt = rmsnorm(hp, norm_final)[:, N_META:]
    y_sample = rmsnorm(hs, norm_final)
    fox_k_prompt = jnp.stack(fk_p, axis=0)
    fox_v_prompt = jnp.stack(fv_p, axis=0)
    fox_logf_prompt = jnp.stack(ff_p, axis=0)
    ret_state_prompt = jnp.stack(rs_p, axis=0)
    fox_k_sample = jnp.stack(fk_s, axis=0)
    fox_v_sample = jnp.stack(fv_s, axis=0)
    fox_logf_sample = jnp.stack(ff_s, axis=0)
    ret_state_sample = jnp.stack(rs_s, axis=0)
    return (y_prompt, y_sample, fox_k_prompt, fox_v_prompt, fox_logf_prompt, ret_state_prompt,
            fox_k_sample, fox_v_sample, fox_logf_sample, ret_state_sample)
```

```python
import functools
import math

import jax
import jax.numpy as jnp
from jax import lax
from jax.experimental import pallas as pl
from jax.experimental.pallas import tpu as pltpu

EPS = 1e-6
NEG_INF = -1e30
ROPE_BASE = 10000.0

LANES = 128
ROW_ALIGN = 256
VMEM_LIMIT_BYTES = 56 * 1024 * 1024

F32 = jnp.float32
BF16 = jnp.bfloat16


def _pick(n, candidates):
    for c in candidates:
        if c <= n and n % c == 0:
            return c
    return n


def _params(semantics):
    return pltpu.CompilerParams(dimension_semantics=semantics, vmem_limit_bytes=VMEM_LIMIT_BYTES)


def _rms_rows(x, g):
    ms = jnp.mean(x * x, axis=-1, keepdims=True)
    return x * lax.rsqrt(ms + EPS) * g


def _log_sigmoid(x):
    return jnp.minimum(x, 0.0) - jnp.log(1.0 + jnp.exp(-jnp.abs(x)))


def _sigmoid(x):
    return 1.0 / (1.0 + jnp.exp(-x))


def _dot(a, b):
    return jnp.dot(a, b, preferred_element_type=F32)


def _dot_nt(a, b):
    return lax.dot_general(a, b, (((1,), (1,)), ((), ())), preferred_element_type=F32)


def _dot_tn(a, b):
    return lax.dot_general(a, b, (((0,), (0,)), ((), ())), preferred_element_type=F32)


def _section_index(sec, nsec):
    return lambda i, j: (i, jnp.clip(j - sec * nsec, 0, nsec - 1))


def _fox_in_kernel(x_ref, g_ref, w_ref, wf_ref, gq_ref, gk_ref, bf_ref,
                   q_ref, k32_ref, v32_ref, kb_ref, vb_ref, sg_ref, lf_ref, xn_ref,
                   *, nsec, dh, q_scale):
    j = pl.program_id(1)

    @pl.when(j == 0)
    def _():
        xn = _rms_rows(x_ref[...], g_ref[...]).astype(BF16)
        xn_ref[...] = xn
        lf_ref[...] = _log_sigmoid(_dot(xn, wf_ref[...]) + bf_ref[...])

    acc = _dot(xn_ref[...], w_ref[...])
    sec = j // nsec
    heads = acc.shape[1] // dh

    @pl.when(sec == 0)
    def _():
        for hh in range(heads):
            sl = slice(hh * dh, (hh + 1) * dh)
            q_ref[:, sl] = (_rms_rows(acc[:, sl], gq_ref[...]) * q_scale).astype(BF16)

    @pl.when(sec == 1)
    def _():
        for hh in range(heads):
            sl = slice(hh * dh, (hh + 1) * dh)
            kn = _rms_rows(acc[:, sl], gk_ref[...])
            k32_ref[:, sl] = kn
            kb_ref[:, sl] = kn.astype(BF16)

    @pl.when(sec == 2)
    def _():
        v32_ref[...] = acc
        vb_ref[...] = acc.astype(BF16)

    @pl.when(sec == 3)
    def _():
        sg_ref[...] = _sigmoid(acc).astype(BF16)


def _fox_in_proj(x, g, w, wf, gq, gk, bf, rows_out, tm):
    m, d = x.shape
    dh = gq.shape[-1]
    tn = _pick(d, (1024, 512, 256, 128))
    nsec = d // tn
    kern = functools.partial(_fox_in_kernel, nsec=nsec, dh=dh, q_scale=dh ** -0.5)
    sec = lambda s: pl.BlockSpec((tm, tn), _section_index(s, nsec))
    const = lambda shape: pl.BlockSpec(shape, lambda i, j: (0, 0))
    return pl.pallas_call(
        kern,
        grid=(m // tm, 4 * nsec),
        in_specs=[
            pl.BlockSpec((tm, d), lambda i, j: (i, 0)),
            const((1, d)),
            pl.BlockSpec((d, tn), lambda i, j: (0, j)),
            const((d, LANES)),
            const((1, dh)),
            const((1, dh)),
            const((1, LANES)),
        ],
        out_specs=[sec(0), sec(1), sec(2), sec(1), sec(2), sec(3),
                   pl.BlockSpec((tm, LANES), lambda i, j: (i, 0))],
        out_shape=[
            jax.ShapeDtypeStruct((m, d), BF16),
            jax.ShapeDtypeStruct((rows_out, d), F32),
            jax.ShapeDtypeStruct((rows_out, d), F32),
            jax.ShapeDtypeStruct((m, d), BF16),
            jax.ShapeDtypeStruct((m, d), BF16),
            jax.ShapeDtypeStruct((m, d), BF16),
            jax.ShapeDtypeStruct((m, LANES), F32),
        ],
        scratch_shapes=[pltpu.VMEM((tm, d), BF16)],
        compiler_params=_params(("arbitrary", "arbitrary")),
        name="fox_in_proj",
    )(x, g, w, wf, gq, gk, bf)


def _cumsum_kernel(lf_ref, c_ref, ct_ref, carry_ref, *, hp):
    @pl.when(pl.program_id(1) == 0)
    def _():
        carry_ref[...] = jnp.zeros_like(carry_ref)

    x = lf_ref[0]
    tr = x.shape[0]
    row = lax.broadcasted_iota(jnp.int32, (tr, tr), 0)
    col = lax.broadcasted_iota(jnp.int32, (tr, tr), 1)
    tri = (col <= row).astype(F32)
    c = jnp.dot(tri, x, preferred_element_type=F32, precision=lax.Precision.HIGHEST) + carry_ref[...]
    c_ref[0] = c
    ct_ref[0, 0] = c.T[:hp, :]
    carry_ref[...] = c[tr - 1:tr, :]


def _cumsum(lf, tr, hp):
    nseq, n, _ = lf.shape
    nblk = n // tr
    return pl.pallas_call(
        functools.partial(_cumsum_kernel, hp=hp),
        grid=(nseq, nblk),
        in_specs=[pl.BlockSpec((1, tr, LANES), lambda s, b: (s, b, 0))],
        out_specs=[pl.BlockSpec((1, tr, LANES), lambda s, b: (s, b, 0)),
                   pl.BlockSpec((1, 1, hp, tr), lambda s, b: (s, b, 0, 0))],
        out_shape=[jax.ShapeDtypeStruct((nseq, n, LANES), F32),
                   jax.ShapeDtypeStruct((nseq, nblk, hp, tr), F32)],
        scratch_shapes=[pltpu.VMEM((1, LANES), F32)],
        compiler_params=_params(("arbitrary", "arbitrary")),
        name="logf_cumsum",
    )(lf)


def _head_column(c_blk, h):
    lane = lax.broadcasted_iota(jnp.int32, c_blk.shape, 1)
    return jnp.sum(jnp.where(lane == h, c_blk, 0.0), axis=-1, keepdims=True)


def _fox_prompt_kernel(q_ref, k_ref, v_ref, c_ref, ct_ref, sg_ref, o_ref, *, tk):
    h = pl.program_id(0)
    qi = pl.program_id(1)
    q = q_ref[...]
    tq, dh = q.shape
    cq = _head_column(c_ref[...], h)
    blocks_per_q = tq // tk

    def step(kb, carry, masked):
        m, l, acc = carry
        start = pl.multiple_of(kb * tk, tk)
        k = k_ref[pl.ds(start, tk), :]
        v = v_ref[pl.ds(start, tk), :]
        ck = ct_ref[kb, pl.ds(h, 1), :]
        s = _dot_nt(q, k) + (cq - ck)
        if masked:
            row = qi * tq + lax.broadcasted_iota(jnp.int32, (tq, tk), 0)
            col = kb * tk + lax.broadcasted_iota(jnp.int32, (tq, tk), 1)
            s = jnp.where(col <= row, s, NEG_INF)
        m_new = jnp.maximum(m, jnp.max(s, axis=-1, keepdims=True))
        alpha = jnp.exp(m - m_new)
        p = jnp.exp(s - m_new)
        l = alpha * l + jnp.sum(p, axis=-1, keepdims=True)
        acc = alpha * acc + _dot(p.astype(BF16), v)
        return m_new, l, acc

    init = (jnp.full((tq, 1), NEG_INF, F32), jnp.zeros((tq, 1), F32), jnp.zeros((tq, dh), F32))
    first_diag = qi * blocks_per_q
    carry = lax.fori_loop(0, first_diag, functools.partial(step, masked=False), init)
    carry = lax.fori_loop(first_diag, first_diag + blocks_per_q, functools.partial(step, masked=True), carry)
    _, l, acc = carry
    o_ref[...] = (acc / l * sg_ref[...].astype(F32)).astype(BF16)


def _fox_prompt_attention(q, kb, vb, c, ct, sg, dh, tq, tk):
    lp, d = q.shape
    heads = d // dh
    nkb, hp, _ = ct.shape
    col = lambda h, i: (i, h)
    return pl.pallas_call(
        functools.partial(_fox_prompt_kernel, tk=tk),
        grid=(heads, lp // tq),
        in_specs=[
            pl.BlockSpec((tq, dh), col),
            pl.BlockSpec((lp, dh), lambda h, i: (0, h)),
            pl.BlockSpec((lp, dh), lambda h, i: (0, h)),
            pl.BlockSpec((tq, LANES), lambda h, i: (i, 0)),
            pl.BlockSpec((nkb, hp, tk), lambda h, i: (0, 0, 0)),
            pl.BlockSpec((tq, dh), col),
        ],
        out_specs=pl.BlockSpec((tq, dh), col),
        out_shape=jax.ShapeDtypeStruct((lp, d), BF16),
        compiler_params=_params(("arbitrary", "arbitrary")),
        name="fox_prompt_attention",
    )(q, kb, vb, c, ct, sg)


def _fox_sample_kernel(q_ref, kc_ref, vc_ref, kn_ref, vn_ref, c_ref, ct_ref, sg_ref, o_ref, *, past):
    h = pl.program_id(1)
    q = q_ref[...]
    t = q.shape[0]
    cq = _head_column(c_ref[0], h)
    ck = ct_ref[0, 0, pl.ds(h, 1), :]
    s_old = _dot_nt(q, kc_ref[0].astype(BF16)) + (cq - ck[:, :past])
    s_new = _dot_nt(q, kn_ref[...]) + (cq - ck[:, past:past + t])
    row = lax.broadcasted_iota(jnp.int32, (t, t), 0)
    col = lax.broadcasted_iota(jnp.int32, (t, t), 1)
    s_new = jnp.where(col <= row, s_new, NEG_INF)
    m = jnp.maximum(jnp.max(s_old, axis=-1, keepdims=True), jnp.max(s_new, axis=-1, keepdims=True))
    p_old = jnp.exp(s_old - m)
    p_new = jnp.exp(s_new - m)
    l = jnp.sum(p_old, axis=-1, keepdims=True) + jnp.sum(p_new, axis=-1, keepdims=True)
    acc = _dot(p_old.astype(BF16), vc_ref[0].astype(BF16)) + _dot(p_new.astype(BF16), vn_ref[...])
    o_ref[...] = (acc / l * sg_ref[...].astype(F32)).astype(BF16)


def _fox_sample_attention(q, k_cache, v_cache, kb, vb, c, ct, sg, dh, t):
    ms, d = q.shape
    nstream, past, _ = k_cache.shape
    heads = d // dh
    _, _, hp, lc = ct.shape
    blk = lambda b, h: (b, h)
    cache = pl.BlockSpec((1, past, dh), lambda b, h: (b, 0, h))
    return pl.pallas_call(
        functools.partial(_fox_sample_kernel, past=past),
        grid=(nstream, heads),
        in_specs=[
            pl.BlockSpec((t, dh), blk),
            cache, cache,
            pl.BlockSpec((t, dh), blk),
            pl.BlockSpec((t, dh), blk),
            pl.BlockSpec((1, t, LANES), lambda b, h: (b, past // t, 0)),
            pl.BlockSpec((1, 1, hp, lc), lambda b, h: (b, 0, 0, 0)),
            pl.BlockSpec((t, dh), blk),
        ],
        out_specs=pl.BlockSpec((t, dh), blk),
        out_shape=jax.ShapeDtypeStruct((ms, d), BF16),
        compiler_params=_params(("arbitrary", "arbitrary")),
        name="fox_sample_attention",
    )(q, k_cache, v_cache, kb, vb, c, ct, sg)


def _out_proj_kernel(x_ref, w_ref, res_ref, o_ref):
    o_ref[...] = res_ref[...] + _dot(x_ref[...], w_ref[...])


def _out_proj(x, w, res, tm):
    m, k = x.shape
    n = w.shape[1]
    tn = _pick(n, (1024, 512, 256, 128))
    return pl.pallas_call(
        _out_proj_kernel,
        grid=(m // tm, n // tn),
        in_specs=[pl.BlockSpec((tm, k), lambda i, j: (i, 0)),
                  pl.BlockSpec((k, tn), lambda i, j: (0, j)),
                  pl.BlockSpec((tm, tn), lambda i, j: (i, j))],
        out_specs=pl.BlockSpec((tm, tn), lambda i, j: (i, j)),
        out_shape=jax.ShapeDtypeStruct((m, n), F32),
        compiler_params=_params(("arbitrary", "arbitrary")),
        name="out_proj_residual",
    )(x, w, res)


def _ret_in_kernel(x_ref, g_ref, w_ref, cos_ref, sin_ref, q_ref, k_ref, v_ref, sg_ref, xn_ref,
                   *, bounds, dk, k_scale):
    j = pl.program_id(1)

    @pl.when(j == 0)
    def _():
        xn_ref[...] = _rms_rows(x_ref[...], g_ref[...]).astype(BF16)

    acc = _dot(xn_ref[...], w_ref[...])
    half = dk // 2
    heads = acc.shape[1] // dk

    def rotary(out_ref, scale):
        cos = cos_ref[...]
        sin = sin_ref[...]
        for hh in range(heads):
            x1 = acc[:, hh * dk:hh * dk + half]
            x2 = acc[:, hh * dk + half:(hh + 1) * dk]
            out_ref[:, hh * dk:hh * dk + half] = ((x1 * cos - x2 * sin) * scale).astype(BF16)
            out_ref[:, hh * dk + half:(hh + 1) * dk] = ((x1 * sin + x2 * cos) * scale).astype(BF16)

    @pl.when(j < bounds[0])
    def _():
        rotary(q_ref, 1.0)

    @pl.when((j >= bounds[0]) & (j < bounds[1]))
    def _():
        rotary(k_ref, k_scale)

    @pl.when((j >= bounds[1]) & (j < bounds[2]))
    def _():
        v_ref[...] = acc.astype(BF16)

    @pl.when(j >= bounds[2])
    def _():
        sg_ref[...] = (acc * _sigmoid(acc)).astype(BF16)


def _ret_in_proj(x, g, w, cos, sin, dqk, dv_total, dk, tm):
    m, d = x.shape
    tn = _pick(math.gcd(dqk, dv_total), (1024, 512, 256))
    nq, nv = dqk // tn, dv_total // tn
    bounds = (nq, 2 * nq, 2 * nq + nv)

    def sec(start, count):
        return pl.BlockSpec((tm, tn), lambda i, j: (i, jnp.clip(j - start, 0, count - 1)))

    kern = functools.partial(_ret_in_kernel, bounds=bounds, dk=dk, k_scale=dk ** -0.5)
    return pl.pallas_call(
        kern,
        grid=(m // tm, 2 * nq + 2 * nv),
        in_specs=[
            pl.BlockSpec((tm, d), lambda i, j: (i, 0)),
            pl.BlockSpec((1, d), lambda i, j: (0, 0)),
            pl.BlockSpec((d, tn), lambda i, j: (0, j)),
            pl.BlockSpec((tm, dk // 2), lambda i, j: (i, 0)),
            pl.BlockSpec((tm, dk // 2), lambda i, j: (i, 0)),
        ],
        out_specs=[sec(0, nq), sec(nq, nq), sec(2 * nq, nv), sec(2 * nq + nv, nv)],
        out_shape=[
            jax.ShapeDtypeStruct((m, dqk), BF16),
            jax.ShapeDtypeStruct((m, dqk), BF16),
            jax.ShapeDtypeStruct((m, dv_total), BF16),
            jax.ShapeDtypeStruct((m, dv_total), BF16),
        ],
        scratch_shapes=[pltpu.VMEM((tm, d), BF16)],
        compiler_params=_params(("arbitrary", "arbitrary")),
        name="ret_in_proj",
    )(x, g, w, cos, sin)


def _retention_kernel(*refs, valid_rows, has_init):
    if has_init:
        q_ref, k_ref, v_ref, sg_ref, gn_ref, lg_ref, s0_ref, y_ref, s_out_ref, s_ref = refs
    else:
        q_ref, k_ref, v_ref, sg_ref, gn_ref, lg_ref, y_ref, s_out_ref, s_ref = refs
    c = pl.program_id(2)
    nchunks = pl.num_programs(2)

    @pl.when(c == 0)
    def _():
        s_ref[...] = s0_ref[0, 0] if has_init else jnp.zeros_like(s_ref)

    q = q_ref[...]
    k = k_ref[...]
    v = v_ref[...]
    rows = q.shape[0]
    lgam = lg_ref[0][:, :1]
    ri = lax.broadcasted_iota(jnp.int32, (rows, rows), 0)
    ci = lax.broadcasted_iota(jnp.int32, (rows, rows), 1)
    diff = (ri - ci).astype(F32)
    decay = jnp.where(diff >= 0, jnp.exp(jnp.maximum(diff, 0.0) * lgam), 0.0)
    scores = _dot_nt(q, k) * decay
    inner = _dot(scores.astype(BF16), v)
    rowf = lax.broadcasted_iota(jnp.int32, (rows, 1), 0).astype(F32)
    state = s_ref[...]
    cross = _dot(q, state.astype(BF16)) * jnp.exp((rowf + 1.0) * lgam)
    o = inner + cross

    nv = jnp.clip(valid_rows - c * rows, 0, rows).astype(F32)
    k_decay = jnp.where(rowf < nv, jnp.exp(jnp.maximum(nv - 1.0 - rowf, 0.0) * lgam), 0.0)
    kd = (k.astype(F32) * k_decay).astype(BF16)
    new_state = jnp.exp(nv * lgam) * state + _dot_tn(kd, v)
    s_ref[...] = new_state

    @pl.when(c == nchunks - 1)
    def _():
        s_out_ref[0, 0] = new_state

    ms = jnp.mean(o * o, axis=-1, keepdims=True)
    on = o * lax.rsqrt(ms + EPS) * gn_ref[0]
    y_ref[...] = (sg_ref[...].astype(F32) * on).astype(BF16)


def _retention(q, k, v, sg, gnorm, log_gamma, s0, nseq, rows, valid_rows, dk, dv):
    m = q.shape[0]
    heads = q.shape[1] // dk
    nchunks = m // (nseq * rows)
    blk = lambda s, h, c: (s * nchunks + c, h)
    per_head = lambda s, h, c: (h, 0, 0)
    state_spec = pl.BlockSpec((1, 1, dk, dv), lambda s, h, c: (s, h, 0, 0))
    in_specs = [
        pl.BlockSpec((rows, dk), blk),
        pl.BlockSpec((rows, dk), blk),
        pl.BlockSpec((rows, dv), blk),
        pl.BlockSpec((rows, dv), blk),
        pl.BlockSpec((1, 1, dv), per_head),
        pl.BlockSpec((1, 1, LANES), per_head),
    ]
    args = [q, k, v, sg, gnorm, log_gamma]
    if s0 is not None:
        in_specs.append(state_spec)
        args.append(s0)
    return pl.pallas_call(
        functools.partial(_retention_kernel, valid_rows=valid_rows, has_init=s0 is not None),
        grid=(nseq, heads, nchunks),
        in_specs=in_specs,
        out_specs=[pl.BlockSpec((rows, dv), blk), state_spec],
        out_shape=[jax.ShapeDtypeStruct((m, heads * dv), BF16),
                   jax.ShapeDtypeStruct((nseq, heads, dk, dv), F32)],
        scratch_shapes=[pltpu.VMEM((dk, dv), F32)],
        compiler_params=_params(("arbitrary", "arbitrary", "arbitrary")),
        name="retention_scan",
    )(*args)


def _mlp_kernel(x_ref, g_ref, wu_ref, wd_ref, o_ref, xn_ref):
    j = pl.program_id(1)

    @pl.when(j == 0)
    def _():
        x = x_ref[...]
        xn_ref[...] = _rms_rows(x, g_ref[...]).astype(BF16)
        o_ref[...] = x

    hid = jnp.maximum(_dot(xn_ref[...], wu_ref[...]), 0.0)
    o_ref[...] += _dot((hid * hid).astype(BF16), wd_ref[...])


def _mlp(x, g, w_up, w_down, tm):
    m, d = x.shape
    f = w_up.shape[1]
    tf = _pick(f, (512, 256, 128))
    return pl.pallas_call(
        _mlp_kernel,
        grid=(m // tm, f // tf),
        in_specs=[pl.BlockSpec((tm, d), lambda i, j: (i, 0)),
                  pl.BlockSpec((1, d), lambda i, j: (0, 0)),
                  pl.BlockSpec((d, tf), lambda i, j: (0, j)),
                  pl.BlockSpec((tf, d), lambda i, j: (j, 0))],
        out_specs=pl.BlockSpec((tm, d), lambda i, j: (i, 0)),
        out_shape=jax.ShapeDtypeStruct((m, d), F32),
        scratch_shapes=[pltpu.VMEM((tm, d), BF16)],
        compiler_params=_params(("arbitrary", "arbitrary")),
        name="sq_relu_mlp",
    )(x, g, w_up, w_down)


def _final_norm_kernel(x_ref, g_ref, o_ref):
    o_ref[...] = _rms_rows(x_ref[...], g_ref[...])


def _final_norm(x, g, tr):
    m, d = x.shape
    return pl.pallas_call(
        _final_norm_kernel,
        grid=(m // tr,),
        in_specs=[pl.BlockSpec((tr, d), lambda i: (i, 0)), pl.BlockSpec((1, d), lambda i: (0, 0))],
        out_specs=pl.BlockSpec((tr, d), lambda i: (i, 0)),
        out_shape=jax.ShapeDtypeStruct((m, d), F32),
        compiler_params=_params(("arbitrary",)),
        name="final_norm",
    )(x, g)


def _final_norm_shift_kernel(xa_ref, xb_ref, g_ref, o_ref, *, shift):
    tr = o_ref.shape[0]
    o_ref[:tr - shift, :] = _rms_rows(xa_ref[shift:, :], g_ref[...])
    o_ref[tr - shift:, :] = _rms_rows(xb_ref[:shift, :], g_ref[...])


def _final_norm_shift(x, g, shift, rows_out, tr):
    d = x.shape[1]
    return pl.pallas_call(
        functools.partial(_final_norm_shift_kernel, shift=shift),
        grid=(rows_out // tr,),
        in_specs=[pl.BlockSpec((tr, d), lambda i: (i, 0)),
                  pl.BlockSpec((tr, d), lambda i: (i + 1, 0)),
                  pl.BlockSpec((1, d), lambda i: (0, 0))],
        out_specs=pl.BlockSpec((tr, d), lambda i: (i, 0)),
        out_shape=jax.ShapeDtypeStruct((rows_out, d), F32),
        compiler_params=_params(("arbitrary",)),
        name="final_norm_prompt",
    )(x, x, g)


def _rotary_tables(pos, half):
    inv = jnp.power(ROPE_BASE, -jnp.arange(half, dtype=F32) / half)
    ang = pos.astype(F32)[:, None] * inv[None, :]
    return jnp.cos(ang), jnp.sin(ang)


def _lane_pad(x, width=LANES):
    return jnp.pad(x, ((0, 0),) * (x.ndim - 1) + ((0, width - x.shape[-1]),))


def kernel(x_prompt, x_sample, cache_fox_k, cache_fox_v, cache_fox_logf, state_ret, meta_tokens,
           norm_mix, norm_mlp, norm_final, fox_w_in, fox_g_q, fox_g_k, fox_b_f, fox_w_o,
           ret_w_in, ret_g_norm, ret_w_o, mlp_w_up, mlp_w_down):
    batch, seq, d = x_prompt.shape
    assert batch == 1, "one prompt stream"
    nstream, t, _ = x_sample.shape
    n_meta = meta_tokens.shape[0]
    depth = norm_mix.shape[0]
    n_fox, _, past, h_fox, dh = cache_fox_k.shape
    n_ret, _, h_ret, dk, dv = state_ret.shape
    dqk, dvt = h_ret * dk, h_ret * dv
    l = n_meta + seq
    lp = -(-l // ROW_ALIGN) * ROW_ALIGN
    ms = nstream * t
    hp = -(-h_fox // 8) * 8
    assert h_fox <= LANES and past % t == 0 and past % LANES == 0
    assert seq % ROW_ALIGN == 0 and 0 < n_meta <= ROW_ALIGN and n_meta % 8 == 0

    tm_p = _pick(lp, (640, 512, 256))
    tm_s = _pick(ms, (512, 256, 128, 64))
    tc = ROW_ALIGN
    tq = _pick(lp, (5 * tc, tc))
    ret_rows = _pick(lp, (256,))

    hp_rows = jnp.concatenate(
        [meta_tokens.astype(x_prompt.dtype), x_prompt[0], jnp.zeros((lp - l, d), x_prompt.dtype)], axis=0)
    hs_rows = x_sample.reshape(ms, d)

    cos_p, sin_p = _rotary_tables(jnp.arange(lp), dk // 2)
    cos_s, sin_s = _rotary_tables(n_meta + past + (jnp.arange(ms) % t), dk // 2)
    log_gamma = jnp.log(1.0 - jnp.power(2.0, -5.0 - jnp.arange(h_ret, dtype=F32)))
    log_gamma = jnp.broadcast_to(log_gamma[:, None, None], (h_ret, 1, LANES))

    lc = -(-(past + t) // LANES) * LANES

    outs = {name: [] for name in ("kp", "vp", "fp", "sp", "ks", "vs", "fs", "ss")}
    row = lambda a: a.reshape(1, -1).astype(F32)
    for i in range(depth):
        j = i // 2
        g_mix = row(norm_mix[i])
        if i % 2 == 0:
            w_main = fox_w_in[j][:, :4 * d].astype(BF16)
            w_f = _lane_pad(fox_w_in[j][:, 4 * d:]).astype(BF16)
            b_f = _lane_pad(row(fox_b_f[j]))
            gq, gk = row(fox_g_q[j]), row(fox_g_k[j])
            w_o = fox_w_o[j].astype(BF16)

            q, k32, v32, kb, vb, sg, lf = _fox_in_proj(hp_rows, g_mix, w_main, w_f, gq, gk, b_f, l, tm_p)
            c, ct = _cumsum(lf[None], tc, hp)
            og = _fox_prompt_attention(q, kb, vb, c[0], ct[0], sg, dh, tq, tc)
            hp_rows = _out_proj(og, w_o, hp_rows, tm_p)
            outs["kp"].append(k32.reshape(1, l, h_fox, dh))
            outs["vp"].append(v32.reshape(1, l, h_fox, dh))
            outs["fp"].append(lf[:l, :h_fox].reshape(1, l, h_fox))

            q, k32, v32, kb, vb, sg, lf = _fox_in_proj(hs_rows, g_mix, w_main, w_f, gq, gk, b_f, ms, tm_s)
            lf_new = lf.reshape(nstream, t, LANES)
            lf_all = jnp.concatenate(
                [_lane_pad(cache_fox_logf[j].astype(F32)), lf_new,
                 jnp.zeros((nstream, lc - past - t, LANES), F32)], axis=1)
            c, ct = _cumsum(lf_all, lc, hp)
            og = _fox_sample_attention(q, cache_fox_k[j].reshape(nstream, past, d),
                                       cache_fox_v[j].reshape(nstream, past, d), kb, vb, c, ct, sg, dh, t)
            hs_rows = _out_proj(og, w_o, hs_rows, tm_s)
            outs["ks"].append(k32.reshape(nstream, t, h_fox, dh))
            outs["vs"].append(v32.reshape(nstream, t, h_fox, dh))
            outs["fs"].append(lf_new[:, :, :h_fox])
        else:
            w_in = ret_w_in[j].astype(BF16)
            w_o = ret_w_o[j].astype(BF16)
            gn = ret_g_norm[j].reshape(h_ret, 1, dv).astype(F32)

            q, k, v, sg = _ret_in_proj(hp_rows, g_mix, w_in, cos_p, sin_p, dqk, dvt, dk, tm_p)
            y, s_fin = _retention(q, k, v, sg, gn, log_gamma, None, 1, ret_rows, l, dk, dv)
            hp_rows = _out_proj(y, w_o, hp_rows, tm_p)
            outs["sp"].append(s_fin.astype(state_ret.dtype))

            q, k, v, sg = _ret_in_proj(hs_rows, g_mix, w_in, cos_s, sin_s, dqk, dvt, dk, tm_s)
            y, s_new = _retention(q, k, v, sg, gn, log_gamma, state_ret[j].astype(F32), nstream, t, t, dk, dv)
            hs_rows = _out_proj(y, w_o, hs_rows, tm_s)
            outs["ss"].append(s_new.astype(state_ret.dtype))

        g_mlp = row(norm_mlp[i])
        w_up = mlp_w_up[i].astype(BF16)
        w_down = mlp_w_down[i].astype(BF16)
        hp_rows = _mlp(hp_rows, g_mlp, w_up, w_down, tm_p)
        hs_rows = _mlp(hs_rows, g_mlp, w_up, w_down, tm_s)

    g_fin = row(norm_final)
    y_prompt = _final_norm_shift(hp_rows, g_fin, n_meta, seq, ROW_ALIGN).reshape(1, seq, d)
    y_sample = _final_norm(hs_rows, g_fin, tm_s).reshape(nstream, t, d)
    stack = lambda name: jnp.stack(outs[name], axis=0)
    return (y_prompt, y_sample, stack("kp"), stack("vp"), stack("fp"), stack("sp"),
            stack("ks"), stack("vs"), stack("fs"), stack("ss"))
```

```python
import functools
import math

import jax
import jax.numpy as jnp
from jax import lax
from jax.experimental import pallas as pl
from jax.experimental.pallas import tpu as pltpu

EPS = 1e-6
NEG_INF = -1e30
ROPE_BASE = 10000.0
LOG2E = math.log2(math.e)
SKIP_LOG = 110.0
FAST_MAX_LOGIT = 60.0
KV_UNROLL = 4

LANES = 128
ROW_ALIGN = 256
VMEM_LIMIT_BYTES = 56 * 1024 * 1024

F32 = jnp.float32
BF16 = jnp.bfloat16


def _pick(n, candidates):
    for c in candidates:
        if c <= n and n % c == 0:
            return c
    return n


def _params(semantics):
    return pltpu.CompilerParams(dimension_semantics=semantics, vmem_limit_bytes=VMEM_LIMIT_BYTES)


def _rms_rows(x, g):
    ms = jnp.mean(x * x, axis=-1, keepdims=True)
    return x * lax.rsqrt(ms + EPS) * g


def _log_sigmoid(x):
    return jnp.minimum(x, 0.0) - jnp.log(1.0 + jnp.exp(-jnp.abs(x)))


def _sigmoid(x):
    return 1.0 / (1.0 + jnp.exp(-x))


def _dot(a, b):
    return jnp.dot(a, b, preferred_element_type=F32)


def _dot_nt(a, b):
    return lax.dot_general(a, b, (((1,), (1,)), ((), ())), preferred_element_type=F32)


def _dot_tn(a, b):
    return lax.dot_general(a, b, (((0,), (0,)), ((), ())), preferred_element_type=F32)


def _section_index(sec, nsec):
    return lambda i, j: (i, jnp.clip(j - sec * nsec, 0, nsec - 1))


def _fox_in_kernel(x_ref, g_ref, w_ref, wf_ref, gq_ref, gk_ref, bf_ref,
                   q_ref, k32_ref, v32_ref, kb_ref, vb_ref, sg_ref, lf_ref, xn_ref,
                   *, nsec, dh, q_scale):
    j = pl.program_id(1)

    @pl.when(j == 0)
    def _():
        xn = _rms_rows(x_ref[...], g_ref[...]).astype(BF16)
        xn_ref[...] = xn
        lf_ref[...] = _log_sigmoid(_dot(xn, wf_ref[...]) + bf_ref[...])

    acc = _dot(xn_ref[...], w_ref[...])
    sec = j // nsec
    heads = acc.shape[1] // dh

    @pl.when(sec == 0)
    def _():
        for hh in range(heads):
            sl = slice(hh * dh, (hh + 1) * dh)
            q_ref[:, sl] = (_rms_rows(acc[:, sl], gq_ref[...]) * q_scale).astype(BF16)

    @pl.when(sec == 1)
    def _():
        for hh in range(heads):
            sl = slice(hh * dh, (hh + 1) * dh)
            kn = _rms_rows(acc[:, sl], gk_ref[...])
            k32_ref[:, sl] = kn
            kb_ref[:, sl] = kn.astype(BF16)

    @pl.when(sec == 2)
    def _():
        v32_ref[...] = acc
        vb_ref[...] = acc.astype(BF16)

    @pl.when(sec == 3)
    def _():
        sg_ref[...] = _sigmoid(acc).astype(BF16)


def _fox_in_proj(x, g, w, wf, gq, gk, bf, rows_out, tm):
    m, d = x.shape
    dh = gq.shape[-1]
    tn = _pick(d, (1024, 512, 256, 128))
    nsec = d // tn
    kern = functools.partial(_fox_in_kernel, nsec=nsec, dh=dh, q_scale=dh ** -0.5 * LOG2E)
    sec = lambda s: pl.BlockSpec((tm, tn), _section_index(s, nsec))
    const = lambda shape: pl.BlockSpec(shape, lambda i, j: (0, 0))
    return pl.pallas_call(
        kern,
        grid=(m // tm, 4 * nsec),
        in_specs=[
            pl.BlockSpec((tm, d), lambda i, j: (i, 0)),
            const((1, d)),
            pl.BlockSpec((d, tn), lambda i, j: (0, j)),
            const((d, LANES)),
            const((1, dh)),
            const((1, dh)),
            const((1, LANES)),
        ],
        out_specs=[sec(0), sec(1), sec(2), sec(1), sec(2), sec(3),
                   pl.BlockSpec((tm, LANES), lambda i, j: (i, 0))],
        out_shape=[
            jax.ShapeDtypeStruct((m, d), BF16),
            jax.ShapeDtypeStruct((rows_out, d), F32),
            jax.ShapeDtypeStruct((rows_out, d), F32),
            jax.ShapeDtypeStruct((m, d), BF16),
            jax.ShapeDtypeStruct((m, d), BF16),
            jax.ShapeDtypeStruct((m, d), BF16),
            jax.ShapeDtypeStruct((m, LANES), F32),
        ],
        scratch_shapes=[pltpu.VMEM((tm, d), BF16)],
        compiler_params=_params(("arbitrary", "arbitrary")),
        name="fox_in_proj",
    )(x, g, w, wf, gq, gk, bf)


def _cumsum_kernel(lf_ref, c_ref, ct_ref, carry_ref, *, hp):
    @pl.when(pl.program_id(1) == 0)
    def _():
        carry_ref[...] = jnp.zeros_like(carry_ref)

    x = lf_ref[0]
    tr = x.shape[0]
    row = lax.broadcasted_iota(jnp.int32, (tr, tr), 0)
    col = lax.broadcasted_iota(jnp.int32, (tr, tr), 1)
    tri = (col <= row).astype(F32)
    c = jnp.dot(tri, x, preferred_element_type=F32, precision=lax.Precision.HIGHEST) + carry_ref[...]
    c_ref[0] = c
    ct_ref[0, 0] = c.T[:hp, :]
    carry_ref[...] = c[tr - 1:tr, :]


def _cumsum(lf, tr, hp):
    nseq, n, _ = lf.shape
    nblk = n // tr
    return pl.pallas_call(
        functools.partial(_cumsum_kernel, hp=hp),
        grid=(nseq, nblk),
        in_specs=[pl.BlockSpec((1, tr, LANES), lambda s, b: (s, b, 0))],
        out_specs=[pl.BlockSpec((1, tr, LANES), lambda s, b: (s, b, 0)),
                   pl.BlockSpec((1, 1, hp, tr), lambda s, b: (s, b, 0, 0))],
        out_shape=[jax.ShapeDtypeStruct((nseq, n, LANES), F32),
                   jax.ShapeDtypeStruct((nseq, nblk, hp, tr), F32)],
        scratch_shapes=[pltpu.VMEM((1, LANES), F32)],
        compiler_params=_params(("arbitrary", "arbitrary")),
        name="logf_cumsum",
    )(lf)


def _head_column(c_blk, h):
    lane = lax.broadcasted_iota(jnp.int32, c_blk.shape, 1)
    return jnp.sum(jnp.where(lane == h, c_blk, 0.0), axis=-1, keepdims=True)


def _fox_prompt_kernel(klo_ref, q_ref, k_ref, v_ref, c_ref, ct_ref, sg_ref, o_ref, vaug_ref, acc_ref, m_ref,
                       *, tk, online):
    h = pl.program_id(0)
    qi = pl.program_id(1)
    nq = pl.num_programs(1)
    tq, dh = q_ref.shape

    @pl.when(qi == 0)
    def _():
        ones_col = (lax.broadcasted_iota(jnp.int32, (tk, dh), 1) == 0).astype(BF16)

        def fill(b, carry):
            start = pl.multiple_of(b * tk, tk)
            vaug_ref[pl.ds(start, tk), :dh] = v_ref[pl.ds(start, tk), :]
            vaug_ref[pl.ds(start, tk), dh:] = ones_col
            return carry

        lax.fori_loop(0, v_ref.shape[0] // tk, fill, 0)

    cq = _head_column(c_ref[...], h) * LOG2E
    blocks_per_q = tq // tk
    acc_ref[...] = jnp.zeros_like(acc_ref)
    if online:
        m_ref[...] = jnp.full_like(m_ref, NEG_INF)

    def visit(kb, chunks, diag_chunk):
        start = pl.multiple_of(kb * tk, tk)
        k = k_ref[pl.ds(start, tk), :]
        v_aug = vaug_ref[pl.ds(start, tk), :]
        ck = ct_ref[kb, pl.ds(h, 1), :] * LOG2E
        for r in chunks:
            rows = slice(r * tk, (r + 1) * tk)
            s = _dot_nt(q_ref[rows, :], k) + (cq[rows] - ck)
            if r == diag_chunk:
                row = lax.broadcasted_iota(jnp.int32, (tk, tk), 0)
                col = lax.broadcasted_iota(jnp.int32, (tk, tk), 1)
                s = jnp.where(col <= row, s, NEG_INF)
            if online:
                m_new = jnp.maximum(m_ref[rows, :], jnp.max(s, axis=-1, keepdims=True))
                acc_ref[rows, :] = (jnp.exp2(m_ref[rows, :] - m_new) * acc_ref[rows, :]
                                    + _dot(jnp.exp2(s - m_new).astype(BF16), v_aug))
                m_ref[rows, :] = m_new
            else:
                acc_ref[rows, :] += _dot(jnp.exp2(s).astype(BF16), v_aug)

    def before_tile(kb, carry):
        visit(kb, range(blocks_per_q), None)
        return carry

    def before_tile_unrolled(t, base):
        for u in range(KV_UNROLL):
            visit(base + t * KV_UNROLL + u, range(blocks_per_q), None)
        return base

    first_diag = qi * blocks_per_q
    lo = klo_ref[h * nq + qi]
    rem = lax.rem(first_diag - lo, KV_UNROLL)
    lax.fori_loop(lo, lo + rem, before_tile, 0)
    lax.fori_loop(0, (first_diag - lo - rem) // KV_UNROLL, before_tile_unrolled, lo + rem)
    for d in range(blocks_per_q):
        visit(first_diag + d, range(d, blocks_per_q), d)
    acc = acc_ref[...]
    o_ref[...] = (acc[:, :dh] / acc[:, dh:dh + 1] * sg_ref[...].astype(F32)).astype(BF16)


def _fox_prompt_attention(klo, q, kb, vb, c, ct, sg, dh, tq, tk, online):
    lp, d = q.shape
    heads = d // dh
    nkb, hp, _ = ct.shape
    col = lambda h, i, klo_ref: (i, h)
    grid_spec = pltpu.PrefetchScalarGridSpec(
        num_scalar_prefetch=1,
        grid=(heads, lp // tq),
        in_specs=[
            pl.BlockSpec((tq, dh), col),
            pl.BlockSpec((lp, dh), lambda h, i, klo_ref: (0, h)),
            pl.BlockSpec((lp, dh), lambda h, i, klo_ref: (0, h)),
            pl.BlockSpec((tq, LANES), lambda h, i, klo_ref: (i, 0)),
            pl.BlockSpec((nkb, hp, tk), lambda h, i, klo_ref: (0, 0, 0)),
            pl.BlockSpec((tq, dh), col),
        ],
        out_specs=pl.BlockSpec((tq, dh), col),
        scratch_shapes=[pltpu.VMEM((lp, 2 * dh), BF16),
                        pltpu.VMEM((tq, 2 * dh), F32),
                        pltpu.VMEM((tq if online else 8, 1), F32)],
    )
    return pl.pallas_call(
        functools.partial(_fox_prompt_kernel, tk=tk, online=online),
        grid_spec=grid_spec,
        out_shape=jax.ShapeDtypeStruct((lp, d), BF16),
        compiler_params=_params(("arbitrary", "arbitrary")),
        name="fox_prompt_attention_online" if online else "fox_prompt_attention",
    )(klo, q, kb, vb, c, ct, sg)


def _kv_skip_plan(c, logit_bound, heads, tq, tk):
    lp = c.shape[0]
    nq, nk = lp // tq, lp // tk
    c_tile = c[::tq, :heads]
    c_blk = c[tk - 1::tk, :heads]
    before = (jnp.arange(nk)[None, :] + 1) * tk <= jnp.arange(nq)[:, None] * tq
    dead = (c_tile[:, None, :] - c_blk[None, :, :] + 2.0 * logit_bound < -SKIP_LOG) & before[:, :, None]
    return jnp.sum(dead, axis=1).astype(jnp.int32).T.reshape(-1)


def _fox_sample_kernel(q_ref, kc_ref, vc_ref, kn_ref, vn_ref, c_ref, ct_ref, sg_ref, o_ref, *, past):
    h = pl.program_id(1)
    q = q_ref[...]
    t = q.shape[0]
    cq = _head_column(c_ref[0], h) * LOG2E
    ck = ct_ref[0, 0, pl.ds(h, 1), :] * LOG2E
    s_old = _dot_nt(q, kc_ref[0].astype(BF16)) + (cq - ck[:, :past])
    s_new = _dot_nt(q, kn_ref[...]) + (cq - ck[:, past:past + t])
    row = lax.broadcasted_iota(jnp.int32, (t, t), 0)
    col = lax.broadcasted_iota(jnp.int32, (t, t), 1)
    s_new = jnp.where(col <= row, s_new, NEG_INF)
    m = jnp.maximum(jnp.max(s_old, axis=-1, keepdims=True), jnp.max(s_new, axis=-1, keepdims=True))
    p_old = jnp.exp2(s_old - m)
    p_new = jnp.exp2(s_new - m)
    l = jnp.sum(p_old, axis=-1, keepdims=True) + jnp.sum(p_new, axis=-1, keepdims=True)
    acc = _dot(p_old.astype(BF16), vc_ref[0].astype(BF16)) + _dot(p_new.astype(BF16), vn_ref[...])
    o_ref[...] = (acc / l * sg_ref[...].astype(F32)).astype(BF16)


def _fox_sample_attention(q, k_cache, v_cache, kb, vb, c, ct, sg, dh, t):
    ms, d = q.shape
    nstream, past, _ = k_cache.shape
    heads = d // dh
    _, _, hp, lc = ct.shape
    blk = lambda b, h: (b, h)
    cache = pl.BlockSpec((1, past, dh), lambda b, h: (b, 0, h))
    return pl.pallas_call(
        functools.partial(_fox_sample_kernel, past=past),
        grid=(nstream, heads),
        in_specs=[
            pl.BlockSpec((t, dh), blk),
            cache, cache,
            pl.BlockSpec((t, dh), blk),
            pl.BlockSpec((t, dh), blk),
            pl.BlockSpec((1, t, LANES), lambda b, h: (b, past // t, 0)),
            pl.BlockSpec((1, 1, hp, lc), lambda b, h: (b, 0, 0, 0)),
            pl.BlockSpec((t, dh), blk),
        ],
        out_specs=pl.BlockSpec((t, dh), blk),
        out_shape=jax.ShapeDtypeStruct((ms, d), BF16),
        compiler_params=_params(("arbitrary", "arbitrary")),
        name="fox_sample_attention",
    )(q, k_cache, v_cache, kb, vb, c, ct, sg)


def _out_proj_kernel(x_ref, w_ref, res_ref, o_ref):
    o_ref[...] = res_ref[...] + _dot(x_ref[...], w_ref[...])


def _out_proj(x, w, res, tm):
    m, k = x.shape
    n = w.shape[1]
    tn = _pick(n, (1024, 512, 256, 128))
    return pl.pallas_call(
        _out_proj_kernel,
        grid=(m // tm, n // tn),
        in_specs=[pl.BlockSpec((tm, k), lambda i, j: (i, 0)),
                  pl.BlockSpec((k, tn), lambda i, j: (0, j)),
                  pl.BlockSpec((tm, tn), lambda i, j: (i, j))],
        out_specs=pl.BlockSpec((tm, tn), lambda i, j: (i, j)),
        out_shape=jax.ShapeDtypeStruct((m, n), F32),
        compiler_params=_params(("arbitrary", "arbitrary")),
        name="out_proj_residual",
    )(x, w, res)


def _ret_in_kernel(x_ref, g_ref, w_ref, cos_ref, sin_ref, q_ref, k_ref, v_ref, sg_ref, xn_ref,
                   *, bounds, dk, k_scale):
    j = pl.program_id(1)

    @pl.when(j == 0)
    def _():
        xn_ref[...] = _rms_rows(x_ref[...], g_ref[...]).astype(BF16)

    acc = _dot(xn_ref[...], w_ref[...])
    half = dk // 2
    heads = acc.shape[1] // dk

    def rotary(out_ref, scale):
        cos = cos_ref[...]
        sin = sin_ref[...]
        for hh in range(heads):
            x1 = acc[:, hh * dk:hh * dk + half]
            x2 = acc[:, hh * dk + half:(hh + 1) * dk]
            out_ref[:, hh * dk:hh * dk + half] = ((x1 * cos - x2 * sin) * scale).astype(BF16)
            out_ref[:, hh * dk + half:(hh + 1) * dk] = ((x1 * sin + x2 * cos) * scale).astype(BF16)

    @pl.when(j < bounds[0])
    def _():
        rotary(q_ref, 1.0)

    @pl.when((j >= bounds[0]) & (j < bounds[1]))
    def _():
        rotary(k_ref, k_scale)

    @pl.when((j >= bounds[1]) & (j < bounds[2]))
    def _():
        v_ref[...] = acc.astype(BF16)

    @pl.when(j >= bounds[2])
    def _():
        sg_ref[...] = (acc * _sigmoid(acc)).astype(BF16)


def _ret_in_proj(x, g, w, cos, sin, dqk, dv_total, dk, tm):
    m, d = x.shape
    tn = _pick(math.gcd(dqk, dv_total), (1024, 512, 256))
    nq, nv = dqk // tn, dv_total // tn
    bounds = (nq, 2 * nq, 2 * nq + nv)

    def sec(start, count):
        return pl.BlockSpec((tm, tn), lambda i, j: (i, jnp.clip(j - start, 0, count - 1)))

    kern = functools.partial(_ret_in_kernel, bounds=bounds, dk=dk, k_scale=dk ** -0.5)
    return pl.pallas_call(
        kern,
        grid=(m // tm, 2 * nq + 2 * nv),
        in_specs=[
            pl.BlockSpec((tm, d), lambda i, j: (i, 0)),
            pl.BlockSpec((1, d), lambda i, j: (0, 0)),
            pl.BlockSpec((d, tn), lambda i, j: (0, j)),
            pl.BlockSpec((tm, dk // 2), lambda i, j: (i, 0)),
            pl.BlockSpec((tm, dk // 2), lambda i, j: (i, 0)),
        ],
        out_specs=[sec(0, nq), sec(nq, nq), sec(2 * nq, nv), sec(2 * nq + nv, nv)],
        out_shape=[
            jax.ShapeDtypeStruct((m, dqk), BF16),
            jax.ShapeDtypeStruct((m, dqk), BF16),
            jax.ShapeDtypeStruct((m, dv_total), BF16),
            jax.ShapeDtypeStruct((m, dv_total), BF16),
        ],
        scratch_shapes=[pltpu.VMEM((tm, d), BF16)],
        compiler_params=_params(("arbitrary", "arbitrary")),
        name="ret_in_proj",
    )(x, g, w, cos, sin)


def _retention_kernel(*refs, valid_rows, has_init):
    if has_init:
        q_ref, k_ref, v_ref, sg_ref, gn_ref, lg_ref, s0_ref, y_ref, s_out_ref, s_ref = refs
    else:
        q_ref, k_ref, v_ref, sg_ref, gn_ref, lg_ref, y_ref, s_out_ref, s_ref = refs
    c = pl.program_id(2)
    nchunks = pl.num_programs(2)

    @pl.when(c == 0)
    def _():
        s_ref[...] = s0_ref[0, 0] if has_init else jnp.zeros_like(s_ref)

    q = q_ref[...]
    k = k_ref[...]
    v = v_ref[...]
    rows = q.shape[0]
    lgam = lg_ref[0][:, :1]
    ri = lax.broadcasted_iota(jnp.int32, (rows, rows), 0)
    ci = lax.broadcasted_iota(jnp.int32, (rows, rows), 1)
    diff = (ri - ci).astype(F32)
    decay = jnp.where(diff >= 0, jnp.exp(jnp.maximum(diff, 0.0) * lgam), 0.0)
    scores = _dot_nt(q, k) * decay
    inner = _dot(scores.astype(BF16), v)
    rowf = lax.broadcasted_iota(jnp.int32, (rows, 1), 0).astype(F32)
    state = s_ref[...]
    cross = _dot(q, state.astype(BF16)) * jnp.exp((rowf + 1.0) * lgam)
    o = inner + cross

    nv = jnp.clip(valid_rows - c * rows, 0, rows).astype(F32)
    k_decay = jnp.where(rowf < nv, jnp.exp(jnp.maximum(nv - 1.0 - rowf, 0.0) * lgam), 0.0)
    kd = (k.astype(F32) * k_decay).astype(BF16)
    new_state = jnp.exp(nv * lgam) * state + _dot_tn(kd, v)
    s_ref[...] = new_state

    @pl.when(c == nchunks - 1)
    def _():
        s_out_ref[0, 0] = new_state

    ms = jnp.mean(o * o, axis=-1, keepdims=True)
    on = o * lax.rsqrt(ms + EPS) * gn_ref[0]
    y_ref[...] = (sg_ref[...].astype(F32) * on).astype(BF16)


def _retention(q, k, v, sg, gnorm, log_gamma, s0, nseq, rows, valid_rows, dk, dv):
    m = q.shape[0]
    heads = q.shape[1] // dk
    nchunks = m // (nseq * rows)
    blk = lambda s, h, c: (s * nchunks + c, h)
    per_head = lambda s, h, c: (h, 0, 0)
    state_spec = pl.BlockSpec((1, 1, dk, dv), lambda s, h, c: (s, h, 0, 0))
    in_specs = [
        pl.BlockSpec((rows, dk), blk),
        pl.BlockSpec((rows, dk), blk),
        pl.BlockSpec((rows, dv), blk),
        pl.BlockSpec((rows, dv), blk),
        pl.BlockSpec((1, 1, dv), per_head),
        pl.BlockSpec((1, 1, LANES), per_head),
    ]
    args = [q, k, v, sg, gnorm, log_gamma]
    if s0 is not None:
        in_specs.append(state_spec)
        args.append(s0)
    return pl.pallas_call(
        functools.partial(_retention_kernel, valid_rows=valid_rows, has_init=s0 is not None),
        grid=(nseq, heads, nchunks),
        in_specs=in_specs,
        out_specs=[pl.BlockSpec((rows, dv), blk), state_spec],
        out_shape=[jax.ShapeDtypeStruct((m, heads * dv), BF16),
                   jax.ShapeDtypeStruct((nseq, heads, dk, dv), F32)],
        scratch_shapes=[pltpu.VMEM((dk, dv), F32)],
        compiler_params=_params(("arbitrary", "arbitrary", "arbitrary")),
        name="retention_scan",
    )(*args)


def _mlp_kernel(x_ref, g_ref, wu_ref, wd_ref, o_ref, xn_ref):
    j = pl.program_id(1)

    @pl.when(j == 0)
    def _():
        x = x_ref[...]
        xn_ref[...] = _rms_rows(x, g_ref[...]).astype(BF16)
        o_ref[...] = x

    hid = jnp.maximum(_dot(xn_ref[...], wu_ref[...]), 0.0)
    o_ref[...] += _dot((hid * hid).astype(BF16), wd_ref[...])


def _mlp(x, g, w_up, w_down, tm):
    m, d = x.shape
    f = w_up.shape[1]
    tf = _pick(f, (512, 256, 128))
    return pl.pallas_call(
        _mlp_kernel,
        grid=(m // tm, f // tf),
        in_specs=[pl.BlockSpec((tm, d), lambda i, j: (i, 0)),
                  pl.BlockSpec((1, d), lambda i, j: (0, 0)),
                  pl.BlockSpec((d, tf), lambda i, j: (0, j)),
                  pl.BlockSpec((tf, d), lambda i, j: (j, 0))],
        out_specs=pl.BlockSpec((tm, d), lambda i, j: (i, 0)),
        out_shape=jax.ShapeDtypeStruct((m, d), F32),
        scratch_shapes=[pltpu.VMEM((tm, d), BF16)],
        compiler_params=_params(("arbitrary", "arbitrary")),
        name="sq_relu_mlp",
    )(x, g, w_up, w_down)


def _final_norm_kernel(x_ref, g_ref, o_ref):
    o_ref[...] = _rms_rows(x_ref[...], g_ref[...])


def _final_norm(x, g, tr):
    m, d = x.shape
    return pl.pallas_call(
        _final_norm_kernel,
        grid=(m // tr,),
        in_specs=[pl.BlockSpec((tr, d), lambda i: (i, 0)), pl.BlockSpec((1, d), lambda i: (0, 0))],
        out_specs=pl.BlockSpec((tr, d), lambda i: (i, 0)),
        out_shape=jax.ShapeDtypeStruct((m, d), F32),
        compiler_params=_params(("arbitrary",)),
        name="final_norm",
    )(x, g)


def _final_norm_shift_kernel(xa_ref, xb_ref, g_ref, o_ref, *, shift):
    tr = o_ref.shape[0]
    o_ref[:tr - shift, :] = _rms_rows(xa_ref[shift:, :], g_ref[...])
    o_ref[tr - shift:, :] = _rms_rows(xb_ref[:shift, :], g_ref[...])


def _final_norm_shift(x, g, shift, rows_out, tr):
    d = x.shape[1]
    return pl.pallas_call(
        functools.partial(_final_norm_shift_kernel, shift=shift),
        grid=(rows_out // tr,),
        in_specs=[pl.BlockSpec((tr, d), lambda i: (i, 0)),
                  pl.BlockSpec((tr, d), lambda i: (i + 1, 0)),
                  pl.BlockSpec((1, d), lambda i: (0, 0))],
        out_specs=pl.BlockSpec((tr, d), lambda i: (i, 0)),
        out_shape=jax.ShapeDtypeStruct((rows_out, d), F32),
        compiler_params=_params(("arbitrary",)),
        name="final_norm_prompt",
    )(x, x, g)


def _rotary_tables(pos, half):
    inv = jnp.power(ROPE_BASE, -jnp.arange(half, dtype=F32) / half)
    ang = pos.astype(F32)[:, None] * inv[None, :]
    return jnp.cos(ang), jnp.sin(ang)


def _lane_pad(x, width=LANES):
    return jnp.pad(x, ((0, 0),) * (x.ndim - 1) + ((0, width - x.shape[-1]),))


def kernel(x_prompt, x_sample, cache_fox_k, cache_fox_v, cache_fox_logf, state_ret, meta_tokens,
           norm_mix, norm_mlp, norm_final, fox_w_in, fox_g_q, fox_g_k, fox_b_f, fox_w_o,
           ret_w_in, ret_g_norm, ret_w_o, mlp_w_up, mlp_w_down):
    batch, seq, d = x_prompt.shape
    assert batch == 1, "one prompt stream"
    nstream, t, _ = x_sample.shape
    n_meta = meta_tokens.shape[0]
    depth = norm_mix.shape[0]
    n_fox, _, past, h_fox, dh = cache_fox_k.shape
    n_ret, _, h_ret, dk, dv = state_ret.shape
    dqk, dvt = h_ret * dk, h_ret * dv
    l = n_meta + seq
    lp = -(-l // ROW_ALIGN) * ROW_ALIGN
    ms = nstream * t
    hp = -(-h_fox // 8) * 8
    assert h_fox <= LANES and past % t == 0 and past % LANES == 0
    assert seq % ROW_ALIGN == 0 and 0 < n_meta <= ROW_ALIGN and n_meta % 8 == 0

    tm_p = _pick(lp, (640, 512, 256))
    tm_s = _pick(ms, (512, 256, 128, 64))
    tc = ROW_ALIGN
    tq = _pick(lp, (5 * tc, tc))
    ret_rows = _pick(lp, (256,))

    hp_rows = jnp.concatenate(
        [meta_tokens.astype(x_prompt.dtype), x_prompt[0], jnp.zeros((lp - l, d), x_prompt.dtype)], axis=0)
    hs_rows = x_sample.reshape(ms, d)

    cos_p, sin_p = _rotary_tables(jnp.arange(lp), dk // 2)
    cos_s, sin_s = _rotary_tables(n_meta + past + (jnp.arange(ms) % t), dk // 2)
    log_gamma = jnp.log(1.0 - jnp.power(2.0, -5.0 - jnp.arange(h_ret, dtype=F32)))
    log_gamma = jnp.broadcast_to(log_gamma[:, None, None], (h_ret, 1, LANES))

    lc = -(-(past + t) // LANES) * LANES

    outs = {name: [] for name in ("kp", "vp", "fp", "sp", "ks", "vs", "fs", "ss")}
    row = lambda a: a.reshape(1, -1).astype(F32)
    for i in range(depth):
        j = i // 2
        g_mix = row(norm_mix[i])
        if i % 2 == 0:
            w_main = fox_w_in[j].astype(BF16)
            w_f = _lane_pad(w_main[:, 4 * d:])
            b_f = _lane_pad(row(fox_b_f[j]))
            gq, gk = row(fox_g_q[j]), row(fox_g_k[j])
            w_o = fox_w_o[j].astype(BF16)

            q, k32, v32, kb, vb, sg, lf = _fox_in_proj(hp_rows, g_mix, w_main, w_f, gq, gk, b_f, l, tm_p)
            c, ct = _cumsum(lf[None], tc, hp)
            logit_bound = 1.02 * math.sqrt(dh) * jnp.max(jnp.abs(gq)) * jnp.max(jnp.abs(gk))
            klo = _kv_skip_plan(c[0], logit_bound, h_fox, tq, tc)
            attend = functools.partial(_fox_prompt_attention, klo, q, kb, vb, c[0], ct[0], sg, dh, tq, tc)
            og = lax.cond(logit_bound <= FAST_MAX_LOGIT,
                          functools.partial(attend, online=False), functools.partial(attend, online=True))
            hp_rows = _out_proj(og, w_o, hp_rows, tm_p)
            outs["kp"].append(k32.reshape(1, l, h_fox, dh))
            outs["vp"].append(v32.reshape(1, l, h_fox, dh))
            outs["fp"].append(lf[:l, :h_fox].reshape(1, l, h_fox))

            q, k32, v32, kb, vb, sg, lf = _fox_in_proj(hs_rows, g_mix, w_main, w_f, gq, gk, b_f, ms, tm_s)
            lf_new = lf.reshape(nstream, t, LANES)
            lf_all = jnp.concatenate(
                [_lane_pad(cache_fox_logf[j].astype(F32)), lf_new,
                 jnp.zeros((nstream, lc - past - t, LANES), F32)], axis=1)
            c, ct = _cumsum(lf_all, lc, hp)
            og = _fox_sample_attention(q, cache_fox_k[j].reshape(nstream, past, d),
                                       cache_fox_v[j].reshape(nstream, past, d), kb, vb, c, ct, sg, dh, t)
            hs_rows = _out_proj(og, w_o, hs_rows, tm_s)
            outs["ks"].append(k32.reshape(nstream, t, h_fox, dh))
            outs["vs"].append(v32.reshape(nstream, t, h_fox, dh))
            outs["fs"].append(lf_new[:, :, :h_fox])
        else:
            w_in = ret_w_in[j].astype(BF16)
            w_o = ret_w_o[j].astype(BF16)
            gn = ret_g_norm[j].reshape(h_ret, 1, dv).astype(F32)

            q, k, v, sg = _ret_in_proj(hp_rows, g_mix, w_in, cos_p, sin_p, dqk, dvt, dk, tm_p)
            y, s_fin = _retention(q, k, v, sg, gn, log_gamma, None, 1, ret_rows, l, dk, dv)
            hp_rows = _out_proj(y, w_o, hp_rows, tm_p)
            outs["sp"].append(s_fin.astype(state_ret.dtype))

            q, k, v, sg = _ret_in_proj(hs_rows, g_mix, w_in, cos_s, sin_s, dqk, dvt, dk, tm_s)
            y, s_new = _retention(q, k, v, sg, gn, log_gamma, state_ret[j].astype(F32), nstream, t, t, dk, dv)
            hs_rows = _out_proj(y, w_o, hs_rows, tm_s)
            outs["ss"].append(s_new.astype(state_ret.dtype))

        g_mlp = row(norm_mlp[i])
        w_up = mlp_w_up[i].astype(BF16)
        w_down = mlp_w_down[i].astype(BF16)
        hp_rows = _mlp(hp_rows, g_mlp, w_up, w_down, tm_p)
        hs_rows = _mlp(hs_rows, g_mlp, w_up, w_down, tm_s)

    g_fin = row(norm_final)
    y_prompt = _final_norm_shift(hp_rows, g_fin, n_meta, seq, ROW_ALIGN).reshape(1, seq, d)
    y_sample = _final_norm(hs_rows, g_fin, tm_s).reshape(nstream, t, d)
    stack = lambda name: jnp.stack(outs[name], axis=0)
    return (y_prompt, y_sample, stack("kp"), stack("vp"), stack("fp"), stack("sp"),
            stack("ks"), stack("vs"), stack("fs"), stack("ss"))
```

```python
import functools
import math

import jax
import jax.numpy as jnp
from jax import lax
from jax.experimental import pallas as pl
from jax.experimental.pallas import tpu as pltpu

EPS = 1e-6
NEG_INF = -1e30
ROPE_BASE = 10000.0
LOG2E = math.log2(math.e)
SKIP_LOG = 110.0
FAST_MAX_LOGIT = 60.0
KV_UNROLL = 4

LANES = 128
ROW_ALIGN = 256
VMEM_LIMIT_BYTES = 56 * 1024 * 1024

F32 = jnp.float32
BF16 = jnp.bfloat16


def _pick(n, candidates):
    for c in candidates:
        if c <= n and n % c == 0:
            return c
    return n


def _params(semantics):
    return pltpu.CompilerParams(dimension_semantics=semantics, vmem_limit_bytes=VMEM_LIMIT_BYTES)


def _rms_rows(x, g):
    ms = jnp.mean(x * x, axis=-1, keepdims=True)
    return x * lax.rsqrt(ms + EPS) * g


def _log_sigmoid(x):
    return jnp.minimum(x, 0.0) - jnp.log(1.0 + jnp.exp(-jnp.abs(x)))


def _sigmoid(x):
    return 1.0 / (1.0 + jnp.exp(-x))


def _dot(a, b):
    return jnp.dot(a, b, preferred_element_type=F32)


def _dot_nt(a, b):
    return lax.dot_general(a, b, (((1,), (1,)), ((), ())), preferred_element_type=F32)


def _dot_tn(a, b):
    return lax.dot_general(a, b, (((0,), (0,)), ((), ())), preferred_element_type=F32)


def _section_index(sec, nsec):
    return lambda i, j: (i, jnp.clip(j - sec * nsec, 0, nsec - 1))


def _fox_in_kernel(x_ref, g_ref, w_ref, wf_ref, gq_ref, gk_ref, bf_ref,
                   q_ref, k32_ref, v32_ref, kb_ref, vb_ref, sg_ref, lf_ref, xn_ref,
                   *, nsec, dh, q_scale):
    j = pl.program_id(1)

    @pl.when(j == 0)
    def _():
        xn = _rms_rows(x_ref[...], g_ref[...]).astype(BF16)
        xn_ref[...] = xn
        lf_ref[...] = _log_sigmoid(_dot(xn, wf_ref[...]) + bf_ref[...])

    acc = _dot(xn_ref[...], w_ref[...])
    sec = j // nsec
    heads = acc.shape[1] // dh

    @pl.when(sec == 0)
    def _():
        for hh in range(heads):
            sl = slice(hh * dh, (hh + 1) * dh)
            q_ref[:, sl] = (_rms_rows(acc[:, sl], gq_ref[...]) * q_scale).astype(BF16)

    @pl.when(sec == 1)
    def _():
        for hh in range(heads):
            sl = slice(hh * dh, (hh + 1) * dh)
            kn = _rms_rows(acc[:, sl], gk_ref[...])
            k32_ref[:, sl] = kn
            kb_ref[:, sl] = kn.astype(BF16)

    @pl.when(sec == 2)
    def _():
        v32_ref[...] = acc
        vb_ref[...] = acc.astype(BF16)

    @pl.when(sec == 3)
    def _():
        sg_ref[...] = _sigmoid(acc).astype(BF16)


def _fox_in_proj(x, g, w, layer, wf, gq, gk, bf, rows_out, tm):
    m, d = x.shape
    dh = gq.shape[-1]
    tn = _pick(d, (1024, 512, 256, 128))
    nsec = d // tn
    kern = functools.partial(_fox_in_kernel, nsec=nsec, dh=dh, q_scale=dh ** -0.5 * LOG2E)
    sec = lambda s: pl.BlockSpec((tm, tn), _section_index(s, nsec))
    const = lambda shape: pl.BlockSpec(shape, lambda i, j: (0, 0))
    return pl.pallas_call(
        kern,
        grid=(m // tm, 4 * nsec),
        in_specs=[
            pl.BlockSpec((tm, d), lambda i, j: (i, 0)),
            const((1, d)),
            pl.BlockSpec((None, d, tn), lambda i, j: (layer, 0, j)),
            const((d, LANES)),
            const((1, dh)),
            const((1, dh)),
            const((1, LANES)),
        ],
        out_specs=[sec(0), sec(1), sec(2), sec(1), sec(2), sec(3),
                   pl.BlockSpec((tm, LANES), lambda i, j: (i, 0))],
        out_shape=[
            jax.ShapeDtypeStruct((m, d), BF16),
            jax.ShapeDtypeStruct((rows_out, d), F32),
            jax.ShapeDtypeStruct((rows_out, d), F32),
            jax.ShapeDtypeStruct((m, d), BF16),
            jax.ShapeDtypeStruct((m, d), BF16),
            jax.ShapeDtypeStruct((m, d), BF16),
            jax.ShapeDtypeStruct((m, LANES), F32),
        ],
        scratch_shapes=[pltpu.VMEM((tm, d), BF16)],
        compiler_params=_params(("arbitrary", "arbitrary")),
        name="fox_in_proj",
    )(x, g, w, wf, gq, gk, bf)


def _cumsum_kernel(lf_ref, c_ref, ct_ref, carry_ref, *, hp):
    @pl.when(pl.program_id(1) == 0)
    def _():
        carry_ref[...] = jnp.zeros_like(carry_ref)

    x = lf_ref[0]
    tr = x.shape[0]
    row = lax.broadcasted_iota(jnp.int32, (tr, tr), 0)
    col = lax.broadcasted_iota(jnp.int32, (tr, tr), 1)
    tri = (col <= row).astype(F32)
    c = jnp.dot(tri, x, preferred_element_type=F32, precision=lax.Precision.HIGHEST) + carry_ref[...]
    c_ref[0] = c
    ct_ref[0, 0] = c.T[:hp, :]
    carry_ref[...] = c[tr - 1:tr, :]


def _cumsum(lf, tr, hp):
    nseq, n, _ = lf.shape
    nblk = n // tr
    return pl.pallas_call(
        functools.partial(_cumsum_kernel, hp=hp),
        grid=(nseq, nblk),
        in_specs=[pl.BlockSpec((1, tr, LANES), lambda s, b: (s, b, 0))],
        out_specs=[pl.BlockSpec((1, tr, LANES), lambda s, b: (s, b, 0)),
                   pl.BlockSpec((1, 1, hp, tr), lambda s, b: (s, b, 0, 0))],
        out_shape=[jax.ShapeDtypeStruct((nseq, n, LANES), F32),
                   jax.ShapeDtypeStruct((nseq, nblk, hp, tr), F32)],
        scratch_shapes=[pltpu.VMEM((1, LANES), F32)],
        compiler_params=_params(("arbitrary", "arbitrary")),
        name="logf_cumsum",
    )(lf)


def _head_column(c_blk, h):
    lane = lax.broadcasted_iota(jnp.int32, c_blk.shape, 1)
    return jnp.sum(jnp.where(lane == h, c_blk, 0.0), axis=-1, keepdims=True)


def _fox_prompt_kernel(klo_ref, q_ref, k_ref, v_ref, c_ref, ct_ref, sg_ref, o_ref, vaug_ref, acc_ref, m_ref,
                       *, tk, online):
    h = pl.program_id(0)
    qi = pl.program_id(1)
    nq = pl.num_programs(1)
    tq, dh = q_ref.shape

    @pl.when(qi == 0)
    def _():
        ones_col = (lax.broadcasted_iota(jnp.int32, (tk, dh), 1) == 0).astype(BF16)

        def fill(b, carry):
            start = pl.multiple_of(b * tk, tk)
            vaug_ref[pl.ds(start, tk), :dh] = v_ref[pl.ds(start, tk), :]
            vaug_ref[pl.ds(start, tk), dh:] = ones_col
            return carry

        lax.fori_loop(0, v_ref.shape[0] // tk, fill, 0)

    cq = _head_column(c_ref[...], h) * LOG2E
    blocks_per_q = tq // tk
    acc_ref[...] = jnp.zeros_like(acc_ref)
    if online:
        m_ref[...] = jnp.full_like(m_ref, NEG_INF)

    def visit(kb, chunks, diag_chunk):
        start = pl.multiple_of(kb * tk, tk)
        k = k_ref[pl.ds(start, tk), :]
        v_aug = vaug_ref[pl.ds(start, tk), :]
        ck = ct_ref[kb, pl.ds(h, 1), :] * LOG2E
        for r in chunks:
            rows = slice(r * tk, (r + 1) * tk)
            s = _dot_nt(q_ref[rows, :], k) + (cq[rows] - ck)
            if r == diag_chunk:
                row = lax.broadcasted_iota(jnp.int32, (tk, tk), 0)
                col = lax.broadcasted_iota(jnp.int32, (tk, tk), 1)
                s = jnp.where(col <= row, s, NEG_INF)
            if online:
                m_new = jnp.maximum(m_ref[rows, :], jnp.max(s, axis=-1, keepdims=True))
                acc_ref[rows, :] = (jnp.exp2(m_ref[rows, :] - m_new) * acc_ref[rows, :]
                                    + _dot(jnp.exp2(s - m_new).astype(BF16), v_aug))
                m_ref[rows, :] = m_new
            else:
                acc_ref[rows, :] += _dot(jnp.exp2(s).astype(BF16), v_aug)

    def before_tile(kb, carry):
        visit(kb, range(blocks_per_q), None)
        return carry

    def before_tile_unrolled(t, base):
        for u in range(KV_UNROLL):
            visit(base + t * KV_UNROLL + u, range(blocks_per_q), None)
        return base

    first_diag = qi * blocks_per_q
    lo = klo_ref[h * nq + qi]
    rem = lax.rem(first_diag - lo, KV_UNROLL)
    lax.fori_loop(lo, lo + rem, before_tile, 0)
    lax.fori_loop(0, (first_diag - lo - rem) // KV_UNROLL, before_tile_unrolled, lo + rem)
    for d in range(blocks_per_q):
        visit(first_diag + d, range(d, blocks_per_q), d)
    acc = acc_ref[...]
    o_ref[...] = (acc[:, :dh] / acc[:, dh:dh + 1] * sg_ref[...].astype(F32)).astype(BF16)


def _fox_prompt_attention(klo, q, kb, vb, c, ct, sg, dh, tq, tk, online):
    lp, d = q.shape
    heads = d // dh
    nkb, hp, _ = ct.shape
    col = lambda h, i, klo_ref: (i, h)
    grid_spec = pltpu.PrefetchScalarGridSpec(
        num_scalar_prefetch=1,
        grid=(heads, lp // tq),
        in_specs=[
            pl.BlockSpec((tq, dh), col),
            pl.BlockSpec((lp, dh), lambda h, i, klo_ref: (0, h)),
            pl.BlockSpec((lp, dh), lambda h, i, klo_ref: (0, h)),
            pl.BlockSpec((tq, LANES), lambda h, i, klo_ref: (i, 0)),
            pl.BlockSpec((nkb, hp, tk), lambda h, i, klo_ref: (0, 0, 0)),
            pl.BlockSpec((tq, dh), col),
        ],
        out_specs=pl.BlockSpec((tq, dh), col),
        scratch_shapes=[pltpu.VMEM((lp, 2 * dh), BF16),
                        pltpu.VMEM((tq, 2 * dh), F32),
                        pltpu.VMEM((tq if online else 8, 1), F32)],
    )
    return pl.pallas_call(
        functools.partial(_fox_prompt_kernel, tk=tk, online=online),
        grid_spec=grid_spec,
        out_shape=jax.ShapeDtypeStruct((lp, d), BF16),
        compiler_params=_params(("arbitrary", "arbitrary")),
        name="fox_prompt_attention_online" if online else "fox_prompt_attention",
    )(klo, q, kb, vb, c, ct, sg)


def _kv_skip_plan(c, logit_bound, heads, tq, tk):
    lp = c.shape[0]
    nq, nk = lp // tq, lp // tk
    c_tile = c[::tq, :heads]
    c_blk = c[tk - 1::tk, :heads]
    before = (jnp.arange(nk)[None, :] + 1) * tk <= jnp.arange(nq)[:, None] * tq
    dead = (c_tile[:, None, :] - c_blk[None, :, :] + 2.0 * logit_bound < -SKIP_LOG) & before[:, :, None]
    return jnp.sum(dead, axis=1).astype(jnp.int32).T.reshape(-1)


def _fox_sample_kernel(q_ref, kc_ref, vc_ref, kn_ref, vn_ref, c_ref, ct_ref, sg_ref, o_ref):
    t = q_ref.shape[0]
    past, heads, dh = kc_ref.shape
    row = lax.broadcasted_iota(jnp.int32, (t, t), 0)
    col = lax.broadcasted_iota(jnp.int32, (t, t), 1)
    c_new = c_ref[0] * LOG2E
    ck_all = ct_ref[0, 0] * LOG2E
    for h in range(heads):
        cols = slice(h * dh, (h + 1) * dh)
        q = q_ref[:, cols]
        cq = c_new[:, h:h + 1]
        ck = ck_all[h:h + 1, :]
        s_old = _dot_nt(q, kc_ref[:, h, :].astype(BF16)) + (cq - ck[:, :past])
        s_new = _dot_nt(q, kn_ref[:, cols]) + (cq - ck[:, past:past + t])
        s_new = jnp.where(col <= row, s_new, NEG_INF)
        m = jnp.maximum(jnp.max(s_old, axis=-1, keepdims=True), jnp.max(s_new, axis=-1, keepdims=True))
        p_old = jnp.exp2(s_old - m)
        p_new = jnp.exp2(s_new - m)
        l = jnp.sum(p_old, axis=-1, keepdims=True) + jnp.sum(p_new, axis=-1, keepdims=True)
        acc = (_dot(p_old.astype(BF16), vc_ref[:, h, :].astype(BF16))
               + _dot(p_new.astype(BF16), vn_ref[:, cols]))
        o_ref[:, cols] = (acc / l * sg_ref[:, cols].astype(F32)).astype(BF16)


def _fox_sample_attention(q, k_cache, v_cache, layer, kb, vb, c, ct, sg, t):
    ms, d = q.shape
    _, nstream, past, heads, dh = k_cache.shape
    _, _, hp, lc = ct.shape
    rows = pl.BlockSpec((t, d), lambda b: (b, 0))
    cache = pl.BlockSpec((None, None, past, heads, dh), lambda b: (layer, b, 0, 0, 0))
    return pl.pallas_call(
        _fox_sample_kernel,
        grid=(nstream,),
        in_specs=[
            rows, cache, cache, rows, rows,
            pl.BlockSpec((1, t, LANES), lambda b: (b, past // t, 0)),
            pl.BlockSpec((1, 1, hp, lc), lambda b: (b, 0, 0, 0)),
            rows,
        ],
        out_specs=rows,
        out_shape=jax.ShapeDtypeStruct((ms, d), BF16),
        compiler_params=_params(("arbitrary",)),
        name="fox_sample_attention",
    )(q, k_cache, v_cache, kb, vb, c, ct, sg)


def _out_proj_kernel(x_ref, w_ref, res_ref, o_ref):
    o_ref[...] = res_ref[...] + _dot(x_ref[...], w_ref[...])


def _out_proj(x, w, layer, res, tm):
    m, k = x.shape
    n = w.shape[2]
    tn = _pick(n, (1024, 512, 256, 128))
    return pl.pallas_call(
        _out_proj_kernel,
        grid=(m // tm, n // tn),
        in_specs=[pl.BlockSpec((tm, k), lambda i, j: (i, 0)),
                  pl.BlockSpec((None, k, tn), lambda i, j: (layer, 0, j)),
                  pl.BlockSpec((tm, tn), lambda i, j: (i, j))],
        out_specs=pl.BlockSpec((tm, tn), lambda i, j: (i, j)),
        out_shape=jax.ShapeDtypeStruct((m, n), F32),
        compiler_params=_params(("arbitrary", "arbitrary")),
        name="out_proj_residual",
    )(x, w, res)


def _ret_in_kernel(x_ref, g_ref, w_ref, cos_ref, sin_ref, q_ref, k_ref, v_ref, sg_ref, xn_ref,
                   *, bounds, dk, k_scale):
    j = pl.program_id(1)

    @pl.when(j == 0)
    def _():
        xn_ref[...] = _rms_rows(x_ref[...], g_ref[...]).astype(BF16)

    acc = _dot(xn_ref[...], w_ref[...])
    half = dk // 2
    heads = acc.shape[1] // dk

    def rotary(out_ref, scale):
        cos = cos_ref[...]
        sin = sin_ref[...]
        for hh in range(heads):
            x1 = acc[:, hh * dk:hh * dk + half]
            x2 = acc[:, hh * dk + half:(hh + 1) * dk]
            out_ref[:, hh * dk:hh * dk + half] = ((x1 * cos - x2 * sin) * scale).astype(BF16)
            out_ref[:, hh * dk + half:(hh + 1) * dk] = ((x1 * sin + x2 * cos) * scale).astype(BF16)

    @pl.when(j < bounds[0])
    def _():
        rotary(q_ref, 1.0)

    @pl.when((j >= bounds[0]) & (j < bounds[1]))
    def _():
        rotary(k_ref, k_scale)

    @pl.when((j >= bounds[1]) & (j < bounds[2]))
    def _():
        v_ref[...] = acc.astype(BF16)

    @pl.when(j >= bounds[2])
    def _():
        sg_ref[...] = (acc * _sigmoid(acc)).astype(BF16)


def _ret_in_proj(x, g, w, layer, cos, sin, dqk, dv_total, dk, tm):
    m, d = x.shape
    tn = _pick(math.gcd(dqk, dv_total), (1024, 512, 256))
    nq, nv = dqk // tn, dv_total // tn
    bounds = (nq, 2 * nq, 2 * nq + nv)

    def sec(start, count):
        return pl.BlockSpec((tm, tn), lambda i, j: (i, jnp.clip(j - start, 0, count - 1)))

    kern = functools.partial(_ret_in_kernel, bounds=bounds, dk=dk, k_scale=dk ** -0.5)
    return pl.pallas_call(
        kern,
        grid=(m // tm, 2 * nq + 2 * nv),
        in_specs=[
            pl.BlockSpec((tm, d), lambda i, j: (i, 0)),
            pl.BlockSpec((1, d), lambda i, j: (0, 0)),
            pl.BlockSpec((None, d, tn), lambda i, j: (layer, 0, j)),
            pl.BlockSpec((tm, dk // 2), lambda i, j: (i, 0)),
            pl.BlockSpec((tm, dk // 2), lambda i, j: (i, 0)),
        ],
        out_specs=[sec(0, nq), sec(nq, nq), sec(2 * nq, nv), sec(2 * nq + nv, nv)],
        out_shape=[
            jax.ShapeDtypeStruct((m, dqk), BF16),
            jax.ShapeDtypeStruct((m, dqk), BF16),
            jax.ShapeDtypeStruct((m, dv_total), BF16),
            jax.ShapeDtypeStruct((m, dv_total), BF16),
        ],
        scratch_shapes=[pltpu.VMEM((tm, d), BF16)],
        compiler_params=_params(("arbitrary", "arbitrary")),
        name="ret_in_proj",
    )(x, g, w, cos, sin)


def _retention_kernel(*refs, valid_rows, has_init, dk, dv):
    if has_init:
        q_ref, k_ref, v_ref, sg_ref, gn_ref, lg_ref, s0_ref, y_ref, s_out_ref, s_ref = refs
    else:
        q_ref, k_ref, v_ref, sg_ref, gn_ref, lg_ref, y_ref, s_out_ref, s_ref = refs
    c = pl.program_id(1)
    nchunks = pl.num_programs(1)
    rows = q_ref.shape[0]
    heads = q_ref.shape[1] // dk

    @pl.when(c == 0)
    def _():
        s_ref[...] = s0_ref[0] if has_init else jnp.zeros_like(s_ref)

    ri = lax.broadcasted_iota(jnp.int32, (rows, rows), 0)
    ci = lax.broadcasted_iota(jnp.int32, (rows, rows), 1)
    diff = (ri - ci).astype(F32)
    causal = diff >= 0
    lag = jnp.maximum(diff, 0.0)
    rowf = lax.broadcasted_iota(jnp.int32, (rows, 1), 0).astype(F32)
    nv = jnp.clip(valid_rows - c * rows, 0, rows).astype(F32)
    valid = rowf < nv
    k_lag = jnp.maximum(nv - 1.0 - rowf, 0.0)

    for h in range(heads):
        qs = slice(h * dk, (h + 1) * dk)
        vs = slice(h * dv, (h + 1) * dv)
        q = q_ref[:, qs]
        k = k_ref[:, qs]
        v = v_ref[:, vs]
        lgam = lg_ref[h][:, :1]
        scores = _dot_nt(q, k) * jnp.where(causal, jnp.exp(lag * lgam), 0.0)
        state = s_ref[h]
        o = _dot(scores.astype(BF16), v) + _dot(q, state.astype(BF16)) * jnp.exp((rowf + 1.0) * lgam)
        kd = (k.astype(F32) * jnp.where(valid, jnp.exp(k_lag * lgam), 0.0)).astype(BF16)
        new_state = jnp.exp(nv * lgam) * state + _dot_tn(kd, v)
        s_ref[h] = new_state

        @pl.when(c == nchunks - 1)
        def _():
            s_out_ref[0, h] = new_state

        ms = jnp.mean(o * o, axis=-1, keepdims=True)
        on = o * lax.rsqrt(ms + EPS) * gn_ref[h]
        y_ref[:, vs] = (sg_ref[:, vs].astype(F32) * on).astype(BF16)


def _retention(q, k, v, sg, gnorm, log_gamma, s0, layer, nseq, rows, valid_rows, dk, dv):
    m = q.shape[0]
    heads = q.shape[1] // dk
    nchunks = m // (nseq * rows)
    blk = lambda s, c: (s * nchunks + c, 0)
    whole = lambda s, c: (0, 0, 0)
    in_specs = [
        pl.BlockSpec((rows, heads * dk), blk),
        pl.BlockSpec((rows, heads * dk), blk),
        pl.BlockSpec((rows, heads * dv), blk),
        pl.BlockSpec((rows, heads * dv), blk),
        pl.BlockSpec((heads, 1, dv), whole),
        pl.BlockSpec((heads, 1, LANES), whole),
    ]
    args = [q, k, v, sg, gnorm, log_gamma]
    if s0 is not None:
        in_specs.append(pl.BlockSpec((None, 1, heads, dk, dv), lambda s, c: (layer, s, 0, 0, 0)))
        args.append(s0)
    return pl.pallas_call(
        functools.partial(_retention_kernel, valid_rows=valid_rows, has_init=s0 is not None, dk=dk, dv=dv),
        grid=(nseq, nchunks),
        in_specs=in_specs,
        out_specs=[pl.BlockSpec((rows, heads * dv), blk),
                   pl.BlockSpec((1, heads, dk, dv), lambda s, c: (s, 0, 0, 0))],
        out_shape=[jax.ShapeDtypeStruct((m, heads * dv), BF16),
                   jax.ShapeDtypeStruct((nseq, heads, dk, dv), F32)],
        scratch_shapes=[pltpu.VMEM((heads, dk, dv), F32)],
        compiler_params=_params(("arbitrary", "arbitrary")),
        name="retention_scan",
    )(*args)


def _mlp_kernel(x_ref, g_ref, wu_ref, wd_ref, o_ref, xn_ref):
    j = pl.program_id(1)

    @pl.when(j == 0)
    def _():
        x = x_ref[...]
        xn_ref[...] = _rms_rows(x, g_ref[...]).astype(BF16)
        o_ref[...] = x

    hid = jnp.maximum(_dot(xn_ref[...], wu_ref[...]), 0.0)
    o_ref[...] += _dot((hid * hid).astype(BF16), wd_ref[...])


def _mlp(x, g, w_up, w_down, layer, tm):
    m, d = x.shape
    f = w_up.shape[2]
    tf = _pick(f, (512, 256, 128))
    return pl.pallas_call(
        _mlp_kernel,
        grid=(m // tm, f // tf),
        in_specs=[pl.BlockSpec((tm, d), lambda i, j: (i, 0)),
                  pl.BlockSpec((1, d), lambda i, j: (0, 0)),
                  pl.BlockSpec((None, d, tf), lambda i, j: (layer, 0, j)),
                  pl.BlockSpec((None, tf, d), lambda i, j: (layer, j, 0))],
        out_specs=pl.BlockSpec((tm, d), lambda i, j: (i, 0)),
        out_shape=jax.ShapeDtypeStruct((m, d), F32),
        scratch_shapes=[pltpu.VMEM((tm, d), BF16)],
        compiler_params=_params(("arbitrary", "arbitrary")),
        name="sq_relu_mlp",
    )(x, g, w_up, w_down)


def _final_norm_kernel(x_ref, g_ref, o_ref):
    o_ref[...] = _rms_rows(x_ref[...], g_ref[...])


def _final_norm(x, g, tr):
    m, d = x.shape
    return pl.pallas_call(
        _final_norm_kernel,
        grid=(m // tr,),
        in_specs=[pl.BlockSpec((tr, d), lambda i: (i, 0)), pl.BlockSpec((1, d), lambda i: (0, 0))],
        out_specs=pl.BlockSpec((tr, d), lambda i: (i, 0)),
        out_shape=jax.ShapeDtypeStruct((m, d), F32),
        compiler_params=_params(("arbitrary",)),
        name="final_norm",
    )(x, g)


def _final_norm_shift_kernel(xa_ref, xb_ref, g_ref, o_ref, *, shift):
    tr = o_ref.shape[0]
    o_ref[:tr - shift, :] = _rms_rows(xa_ref[shift:, :], g_ref[...])
    o_ref[tr - shift:, :] = _rms_rows(xb_ref[:shift, :], g_ref[...])


def _final_norm_shift(x, g, shift, rows_out, tr):
    d = x.shape[1]
    return pl.pallas_call(
        functools.partial(_final_norm_shift_kernel, shift=shift),
        grid=(rows_out // tr,),
        in_specs=[pl.BlockSpec((tr, d), lambda i: (i, 0)),
                  pl.BlockSpec((tr, d), lambda i: (i + 1, 0)),
                  pl.BlockSpec((1, d), lambda i: (0, 0))],
        out_specs=pl.BlockSpec((tr, d), lambda i: (i, 0)),
        out_shape=jax.ShapeDtypeStruct((rows_out, d), F32),
        compiler_params=_params(("arbitrary",)),
        name="final_norm_prompt",
    )(x, x, g)


def _rotary_tables(pos, half):
    inv = jnp.power(ROPE_BASE, -jnp.arange(half, dtype=F32) / half)
    ang = pos.astype(F32)[:, None] * inv[None, :]
    return jnp.cos(ang), jnp.sin(ang)


def _lane_pad(x, width=LANES):
    return jnp.pad(x, ((0, 0),) * (x.ndim - 1) + ((0, width - x.shape[-1]),))


def kernel(x_prompt, x_sample, cache_fox_k, cache_fox_v, cache_fox_logf, state_ret, meta_tokens,
           norm_mix, norm_mlp, norm_final, fox_w_in, fox_g_q, fox_g_k, fox_b_f, fox_w_o,
           ret_w_in, ret_g_norm, ret_w_o, mlp_w_up, mlp_w_down):
    batch, seq, d = x_prompt.shape
    assert batch == 1, "one prompt stream"
    nstream, t, _ = x_sample.shape
    n_meta = meta_tokens.shape[0]
    depth = norm_mix.shape[0]
    n_fox, _, past, h_fox, dh = cache_fox_k.shape
    n_ret, _, h_ret, dk, dv = state_ret.shape
    dqk, dvt = h_ret * dk, h_ret * dv
    l = n_meta + seq
    lp = -(-l // ROW_ALIGN) * ROW_ALIGN
    ms = nstream * t
    hp = -(-h_fox // 8) * 8
    assert h_fox <= LANES and past % t == 0 and past % LANES == 0
    assert seq % ROW_ALIGN == 0 and 0 < n_meta <= ROW_ALIGN and n_meta % 8 == 0

    tm_p = _pick(lp, (640, 512, 256))
    tm_s = _pick(ms, (512, 256, 128, 64))
    tc = ROW_ALIGN
    tq = _pick(lp, (5 * tc, tc))
    ret_rows = _pick(lp, (256,))

    hp_rows = jnp.concatenate(
        [meta_tokens.astype(x_prompt.dtype), x_prompt[0], jnp.zeros((lp - l, d), x_prompt.dtype)], axis=0)
    hs_rows = x_sample.reshape(ms, d)

    cos_p, sin_p = _rotary_tables(jnp.arange(lp), dk // 2)
    cos_s, sin_s = _rotary_tables(n_meta + past + (jnp.arange(ms) % t), dk // 2)
    log_gamma = jnp.log(1.0 - jnp.power(2.0, -5.0 - jnp.arange(h_ret, dtype=F32)))
    log_gamma = jnp.broadcast_to(log_gamma[:, None, None], (h_ret, 1, LANES))

    lc = -(-(past + t) // LANES) * LANES

    outs = {name: [] for name in ("kp", "vp", "fp", "sp", "ks", "vs", "fs", "ss")}
    row = lambda a: a.reshape(1, -1).astype(F32)
    fox_w_in16, fox_w_o16 = fox_w_in.astype(BF16), fox_w_o.astype(BF16)
    ret_w_in16, ret_w_o16 = ret_w_in.astype(BF16), ret_w_o.astype(BF16)
    mlp_w_up16, mlp_w_down16 = mlp_w_up.astype(BF16), mlp_w_down.astype(BF16)
    state_f32 = state_ret.astype(F32)
    for i in range(depth):
        j = i // 2
        g_mix = row(norm_mix[i])
        if i % 2 == 0:
            w_f = _lane_pad(fox_w_in16[j][:, 4 * d:])
            b_f = _lane_pad(row(fox_b_f[j]))
            gq, gk = row(fox_g_q[j]), row(fox_g_k[j])

            q, k32, v32, kb, vb, sg, lf = _fox_in_proj(hp_rows, g_mix, fox_w_in16, j, w_f, gq, gk, b_f, l, tm_p)
            c, ct = _cumsum(lf[None], tc, hp)
            logit_bound = 1.02 * math.sqrt(dh) * jnp.max(jnp.abs(gq)) * jnp.max(jnp.abs(gk))
            klo = _kv_skip_plan(c[0], logit_bound, h_fox, tq, tc)
            attend = functools.partial(_fox_prompt_attention, klo, q, kb, vb, c[0], ct[0], sg, dh, tq, tc)
            og = lax.cond(logit_bound <= FAST_MAX_LOGIT,
                          functools.partial(attend, online=False), functools.partial(attend, online=True))
            hp_rows = _out_proj(og, fox_w_o16, j, hp_rows, tm_p)
            outs["kp"].append(k32.reshape(1, l, h_fox, dh))
            outs["vp"].append(v32.reshape(1, l, h_fox, dh))
            outs["fp"].append(lf[:l, :h_fox].reshape(1, l, h_fox))

            q, k32, v32, kb, vb, sg, lf = _fox_in_proj(hs_rows, g_mix, fox_w_in16, j, w_f, gq, gk, b_f, ms, tm_s)
            lf_new = lf.reshape(nstream, t, LANES)
            lf_all = jnp.concatenate(
                [_lane_pad(cache_fox_logf[j].astype(F32)), lf_new,
                 jnp.zeros((nstream, lc - past - t, LANES), F32)], axis=1)
            c, ct = _cumsum(lf_all, lc, hp)
            og = _fox_sample_attention(q, cache_fox_k, cache_fox_v, j, kb, vb, c, ct, sg, t)
            hs_rows = _out_proj(og, fox_w_o16, j, hs_rows, tm_s)
            outs["ks"].append(k32.reshape(nstream, t, h_fox, dh))
            outs["vs"].append(v32.reshape(nstream, t, h_fox, dh))
            outs["fs"].append(lf_new[:, :, :h_fox])
        else:
            gn = ret_g_norm[j].reshape(h_ret, 1, dv).astype(F32)

            q, k, v, sg = _ret_in_proj(hp_rows, g_mix, ret_w_in16, j, cos_p, sin_p, dqk, dvt, dk, tm_p)
            y, s_fin = _retention(q, k, v, sg, gn, log_gamma, None, j, 1, ret_rows, l, dk, dv)
            hp_rows = _out_proj(y, ret_w_o16, j, hp_rows, tm_p)
            outs["sp"].append(s_fin.astype(state_ret.dtype))

            q, k, v, sg = _ret_in_proj(hs_rows, g_mix, ret_w_in16, j, cos_s, sin_s, dqk, dvt, dk, tm_s)
            y, s_new = _retention(q, k, v, sg, gn, log_gamma, state_f32, j, nstream, t, t, dk, dv)
            hs_rows = _out_proj(y, ret_w_o16, j, hs_rows, tm_s)
            outs["ss"].append(s_new.astype(state_ret.dtype))

        g_mlp = row(norm_mlp[i])
        hp_rows = _mlp(hp_rows, g_mlp, mlp_w_up16, mlp_w_down16, i, tm_p)
        hs_rows = _mlp(hs_rows, g_mlp, mlp_w_up16, mlp_w_down16, i, tm_s)

    g_fin = row(norm_final)
    y_prompt = _final_norm_shift(hp_rows, g_fin, n_meta, seq, ROW_ALIGN).reshape(1, seq, d)
    y_sample = _final_norm(hs_rows, g_fin, tm_s).reshape(nstream, t, d)
    stack = lambda name: jnp.stack(outs[name], axis=0)
    return (y_prompt, y_sample, stack("kp"), stack("vp"), stack("fp"), stack("sp"),
            stack("ks"), stack("vs"), stack("fs"), stack("ss"))
```

```python
import functools
import math

import jax
import jax.numpy as jnp
from jax import lax
from jax.experimental import pallas as pl
from jax.experimental.pallas import tpu as pltpu

EPS = 1e-6
NEG_INF = -1e30
ROPE_BASE = 10000.0
LOG2E = math.log2(math.e)
SKIP_LOG = 110.0
FAST_MAX_LOGIT = 60.0
KV_UNROLL = 4

LANES = 128
ROW_ALIGN = 256
VMEM_LIMIT_BYTES = 56 * 1024 * 1024

F32 = jnp.float32
BF16 = jnp.bfloat16


def _pick(n, candidates):
    for c in candidates:
        if c <= n and n % c == 0:
            return c
    return n


def _params(semantics):
    return pltpu.CompilerParams(dimension_semantics=semantics, vmem_limit_bytes=VMEM_LIMIT_BYTES)


def _rms_rows(x, g):
    ms = jnp.mean(x * x, axis=-1, keepdims=True)
    return x * lax.rsqrt(ms + EPS) * g


def _log_sigmoid(x):
    return jnp.minimum(x, 0.0) - jnp.log(1.0 + jnp.exp(-jnp.abs(x)))


def _sigmoid(x):
    return 1.0 / (1.0 + jnp.exp(-x))


def _dot(a, b):
    return jnp.dot(a, b, preferred_element_type=F32)


def _dot_nt(a, b):
    return lax.dot_general(a, b, (((1,), (1,)), ((), ())), preferred_element_type=F32)


def _dot_tn(a, b):
    return lax.dot_general(a, b, (((0,), (0,)), ((), ())), preferred_element_type=F32)


def _section_index(sec, nsec):
    return lambda i, j: (i, jnp.clip(j - sec * nsec, 0, nsec - 1))


def _fox_in_kernel(x_ref, g_ref, w_ref, wf_ref, gq_ref, gk_ref, bf_ref,
                   q_ref, k32_ref, v32_ref, kb_ref, vb_ref, sg_ref, lf_ref, xn_ref,
                   *, nsec, dh, q_scale):
    j = pl.program_id(1)

    @pl.when(j == 0)
    def _():
        xn = _rms_rows(x_ref[...], g_ref[...]).astype(BF16)
        xn_ref[...] = xn
        lf_ref[...] = _log_sigmoid(_dot(xn, wf_ref[...]) + bf_ref[...])

    acc = _dot(xn_ref[...], w_ref[...])
    sec = j // nsec
    heads = acc.shape[1] // dh

    @pl.when(sec == 0)
    def _():
        for hh in range(heads):
            sl = slice(hh * dh, (hh + 1) * dh)
            q_ref[:, sl] = (_rms_rows(acc[:, sl], gq_ref[...]) * q_scale).astype(BF16)

    @pl.when(sec == 1)
    def _():
        for hh in range(heads):
            sl = slice(hh * dh, (hh + 1) * dh)
            kn = _rms_rows(acc[:, sl], gk_ref[...])
            k32_ref[:, sl] = kn
            kb_ref[:, sl] = kn.astype(BF16)

    @pl.when(sec == 2)
    def _():
        v32_ref[...] = acc
        vb_ref[...] = acc.astype(BF16)

    @pl.when(sec == 3)
    def _():
        sg_ref[...] = _sigmoid(acc).astype(BF16)


def _fox_in_proj(x, g, w, layer, wf, gq, gk, bf, rows_out, tm):
    m, d = x.shape
    dh = gq.shape[-1]
    tn = _pick(d, (1024, 512, 256, 128))
    nsec = d // tn
    kern = functools.partial(_fox_in_kernel, nsec=nsec, dh=dh, q_scale=dh ** -0.5 * LOG2E)
    sec = lambda s: pl.BlockSpec((tm, tn), _section_index(s, nsec))
    const = lambda shape: pl.BlockSpec(shape, lambda i, j: (0, 0))
    return pl.pallas_call(
        kern,
        grid=(m // tm, 4 * nsec),
        in_specs=[
            pl.BlockSpec((tm, d), lambda i, j: (i, 0)),
            const((1, d)),
            pl.BlockSpec((None, d, tn), lambda i, j: (layer, 0, j)),
            const((d, LANES)),
            const((1, dh)),
            const((1, dh)),
            const((1, LANES)),
        ],
        out_specs=[sec(0), sec(1), sec(2), sec(1), sec(2), sec(3),
                   pl.BlockSpec((tm, LANES), lambda i, j: (i, 0))],
        out_shape=[
            jax.ShapeDtypeStruct((m, d), BF16),
            jax.ShapeDtypeStruct((rows_out, d), F32),
            jax.ShapeDtypeStruct((rows_out, d), F32),
            jax.ShapeDtypeStruct((m, d), BF16),
            jax.ShapeDtypeStruct((m, d), BF16),
            jax.ShapeDtypeStruct((m, d), BF16),
            jax.ShapeDtypeStruct((m, LANES), F32),
        ],
        scratch_shapes=[pltpu.VMEM((tm, d), BF16)],
        compiler_params=_params(("arbitrary", "arbitrary")),
        name="fox_in_proj",
    )(x, g, w, wf, gq, gk, bf)


def _cumsum_kernel(lf_ref, c_ref, ct_ref, carry_ref, *, hp, tr, nb):
    @pl.when(pl.program_id(1) == 0)
    def _():
        carry_ref[...] = jnp.zeros_like(carry_ref)

    row = lax.broadcasted_iota(jnp.int32, (tr, tr), 0)
    col = lax.broadcasted_iota(jnp.int32, (tr, tr), 1)
    tri = (col <= row).astype(F32)
    carry = carry_ref[...]
    for r in range(nb):
        rows = slice(r * tr, (r + 1) * tr)
        c = jnp.dot(tri, lf_ref[0, rows, :], preferred_element_type=F32,
                    precision=lax.Precision.HIGHEST) + carry
        c_ref[0, rows, :] = c
        ct_ref[0, r] = c.T[:hp, :]
        carry = c[tr - 1:tr, :]
    carry_ref[...] = carry


def _cumsum(lf, tr, nb, hp):
    nseq, n, _ = lf.shape
    nblk = n // tr
    return pl.pallas_call(
        functools.partial(_cumsum_kernel, hp=hp, tr=tr, nb=nb),
        grid=(nseq, nblk // nb),
        in_specs=[pl.BlockSpec((1, nb * tr, LANES), lambda s, b: (s, b, 0))],
        out_specs=[pl.BlockSpec((1, nb * tr, LANES), lambda s, b: (s, b, 0)),
                   pl.BlockSpec((1, nb, hp, tr), lambda s, b: (s, b, 0, 0))],
        out_shape=[jax.ShapeDtypeStruct((nseq, n, LANES), F32),
                   jax.ShapeDtypeStruct((nseq, nblk, hp, tr), F32)],
        scratch_shapes=[pltpu.VMEM((1, LANES), F32)],
        compiler_params=_params(("arbitrary", "arbitrary")),
        name="logf_cumsum",
    )(lf)


def _head_column(c_blk, h):
    lane = lax.broadcasted_iota(jnp.int32, c_blk.shape, 1)
    return jnp.sum(jnp.where(lane == h, c_blk, 0.0), axis=-1, keepdims=True)


def _fox_prompt_kernel(klo_ref, q_ref, k_ref, v_ref, c_ref, ct_ref, sg_ref, o_ref, vaug_ref, acc_ref, m_ref,
                       *, tk, online):
    h = pl.program_id(0)
    qi = pl.program_id(1)
    nq = pl.num_programs(1)
    tq, dh = q_ref.shape

    @pl.when(qi == 0)
    def _():
        ones_col = (lax.broadcasted_iota(jnp.int32, (tk, dh), 1) == 0).astype(BF16)

        def fill(b, carry):
            start = pl.multiple_of(b * tk, tk)
            vaug_ref[pl.ds(start, tk), :dh] = v_ref[pl.ds(start, tk), :]
            vaug_ref[pl.ds(start, tk), dh:] = ones_col
            return carry

        lax.fori_loop(0, v_ref.shape[0] // tk, fill, 0)

    cq = _head_column(c_ref[...], h) * LOG2E
    blocks_per_q = tq // tk
    acc_ref[...] = jnp.zeros_like(acc_ref)
    if online:
        m_ref[...] = jnp.full_like(m_ref, NEG_INF)

    def visit(kb, chunks, diag_chunk):
        start = pl.multiple_of(kb * tk, tk)
        k = k_ref[pl.ds(start, tk), :]
        v_aug = vaug_ref[pl.ds(start, tk), :]
        ck = ct_ref[kb, pl.ds(h, 1), :] * LOG2E
        for r in chunks:
            rows = slice(r * tk, (r + 1) * tk)
            s = _dot_nt(q_ref[rows, :], k) + (cq[rows] - ck)
            if r == diag_chunk:
                row = lax.broadcasted_iota(jnp.int32, (tk, tk), 0)
                col = lax.broadcasted_iota(jnp.int32, (tk, tk), 1)
                s = jnp.where(col <= row, s, NEG_INF)
            if online:
                m_new = jnp.maximum(m_ref[rows, :], jnp.max(s, axis=-1, keepdims=True))
                acc_ref[rows, :] = (jnp.exp2(m_ref[rows, :] - m_new) * acc_ref[rows, :]
                                    + _dot(jnp.exp2(s - m_new).astype(BF16), v_aug))
                m_ref[rows, :] = m_new
            else:
                acc_ref[rows, :] += _dot(jnp.exp2(s).astype(BF16), v_aug)

    def before_tile(kb, carry):
        visit(kb, range(blocks_per_q), None)
        return carry

    def before_tile_unrolled(t, base):
        for u in range(KV_UNROLL):
            visit(base + t * KV_UNROLL + u, range(blocks_per_q), None)
        return base

    first_diag = qi * blocks_per_q
    lo = klo_ref[h * nq + qi]
    rem = lax.rem(first_diag - lo, KV_UNROLL)
    lax.fori_loop(lo, lo + rem, before_tile, 0)
    lax.fori_loop(0, (first_diag - lo - rem) // KV_UNROLL, before_tile_unrolled, lo + rem)
    for d in range(blocks_per_q):
        visit(first_diag + d, range(d, blocks_per_q), d)
    acc = acc_ref[...]
    o_ref[...] = (acc[:, :dh] / acc[:, dh:dh + 1] * sg_ref[...].astype(F32)).astype(BF16)


def _fox_prompt_attention(klo, q, kb, vb, c, ct, sg, dh, tq, tk, online):
    lp, d = q.shape
    heads = d // dh
    nkb, hp, _ = ct.shape
    col = lambda h, i, klo_ref: (i, h)
    grid_spec = pltpu.PrefetchScalarGridSpec(
        num_scalar_prefetch=1,
        grid=(heads, lp // tq),
        in_specs=[
            pl.BlockSpec((tq, dh), col),
            pl.BlockSpec((lp, dh), lambda h, i, klo_ref: (0, h)),
            pl.BlockSpec((lp, dh), lambda h, i, klo_ref: (0, h)),
            pl.BlockSpec((tq, LANES), lambda h, i, klo_ref: (i, 0)),
            pl.BlockSpec((nkb, hp, tk), lambda h, i, klo_ref: (0, 0, 0)),
            pl.BlockSpec((tq, dh), col),
        ],
        out_specs=pl.BlockSpec((tq, dh), col),
        scratch_shapes=[pltpu.VMEM((lp, 2 * dh), BF16),
                        pltpu.VMEM((tq, 2 * dh), F32),
                        pltpu.VMEM((tq if online else 8, 1), F32)],
    )
    return pl.pallas_call(
        functools.partial(_fox_prompt_kernel, tk=tk, online=online),
        grid_spec=grid_spec,
        out_shape=jax.ShapeDtypeStruct((lp, d), BF16),
        compiler_params=_params(("arbitrary", "arbitrary")),
        name="fox_prompt_attention_online" if online else "fox_prompt_attention",
    )(klo, q, kb, vb, c, ct, sg)


def _kv_skip_plan(c, logit_bound, heads, tq, tk):
    lp = c.shape[0]
    nq, nk = lp // tq, lp // tk
    c_tile = c[::tq, :heads]
    c_blk = c[tk - 1::tk, :heads]
    before = (jnp.arange(nk)[None, :] + 1) * tk <= jnp.arange(nq)[:, None] * tq
    dead = (c_tile[:, None, :] - c_blk[None, :, :] + 2.0 * logit_bound < -SKIP_LOG) & before[:, :, None]
    return jnp.sum(dead, axis=1).astype(jnp.int32).T.reshape(-1)


def _fox_sample_kernel(q_ref, kc_ref, vc_ref, kn_ref, vn_ref, c_ref, ct_ref, sg_ref, o_ref):
    t = q_ref.shape[0]
    past, heads, dh = kc_ref.shape
    row = lax.broadcasted_iota(jnp.int32, (t, t), 0)
    col = lax.broadcasted_iota(jnp.int32, (t, t), 1)
    c_new = c_ref[0] * LOG2E
    ck_all = ct_ref[0, 0] * LOG2E
    for h in range(heads):
        cols = slice(h * dh, (h + 1) * dh)
        q = q_ref[:, cols]
        cq = c_new[:, h:h + 1]
        ck = ck_all[h:h + 1, :]
        s_old = _dot_nt(q, kc_ref[:, h, :].astype(BF16)) + (cq - ck[:, :past])
        s_new = _dot_nt(q, kn_ref[:, cols]) + (cq - ck[:, past:past + t])
        s_new = jnp.where(col <= row, s_new, NEG_INF)
        m = jnp.maximum(jnp.max(s_old, axis=-1, keepdims=True), jnp.max(s_new, axis=-1, keepdims=True))
        p_old = jnp.exp2(s_old - m)
        p_new = jnp.exp2(s_new - m)
        l = jnp.sum(p_old, axis=-1, keepdims=True) + jnp.sum(p_new, axis=-1, keepdims=True)
        acc = (_dot(p_old.astype(BF16), vc_ref[:, h, :].astype(BF16))
               + _dot(p_new.astype(BF16), vn_ref[:, cols]))
        o_ref[:, cols] = (acc / l * sg_ref[:, cols].astype(F32)).astype(BF16)


def _fox_sample_attention(q, k_cache, v_cache, layer, kb, vb, c, ct, sg, t):
    ms, d = q.shape
    _, nstream, past, heads, dh = k_cache.shape
    _, _, hp, lc = ct.shape
    rows = pl.BlockSpec((t, d), lambda b: (b, 0))
    cache = pl.BlockSpec((None, None, past, heads, dh), lambda b: (layer, b, 0, 0, 0))
    return pl.pallas_call(
        _fox_sample_kernel,
        grid=(nstream,),
        in_specs=[
            rows, cache, cache, rows, rows,
            pl.BlockSpec((1, t, LANES), lambda b: (b, past // t, 0)),
            pl.BlockSpec((1, 1, hp, lc), lambda b: (b, 0, 0, 0)),
            rows,
        ],
        out_specs=rows,
        out_shape=jax.ShapeDtypeStruct((ms, d), BF16),
        compiler_params=_params(("arbitrary",)),
        name="fox_sample_attention",
    )(q, k_cache, v_cache, kb, vb, c, ct, sg)


def _out_proj_kernel(x_ref, w_ref, res_ref, o_ref):
    o_ref[...] = res_ref[...] + _dot(x_ref[...], w_ref[...])


def _out_proj(x, w, layer, res, tm):
    m, k = x.shape
    n = w.shape[2]
    tn = _pick(n, (1024, 512, 256, 128))
    return pl.pallas_call(
        _out_proj_kernel,
        grid=(m // tm, n // tn),
        in_specs=[pl.BlockSpec((tm, k), lambda i, j: (i, 0)),
                  pl.BlockSpec((None, k, tn), lambda i, j: (layer, 0, j)),
                  pl.BlockSpec((tm, tn), lambda i, j: (i, j))],
        out_specs=pl.BlockSpec((tm, tn), lambda i, j: (i, j)),
        out_shape=jax.ShapeDtypeStruct((m, n), F32),
        compiler_params=_params(("arbitrary", "arbitrary")),
        name="out_proj_residual",
    )(x, w, res)


def _ret_in_kernel(x_ref, g_ref, w_ref, cos_ref, sin_ref, q_ref, k_ref, v_ref, sg_ref, xn_ref,
                   *, bounds, dk, k_scale):
    j = pl.program_id(1)

    @pl.when(j == 0)
    def _():
        xn_ref[...] = _rms_rows(x_ref[...], g_ref[...]).astype(BF16)

    acc = _dot(xn_ref[...], w_ref[...])
    half = dk // 2
    heads = acc.shape[1] // dk

    def rotary(out_ref, scale):
        cos = cos_ref[...]
        sin = sin_ref[...]
        for hh in range(heads):
            x1 = acc[:, hh * dk:hh * dk + half]
            x2 = acc[:, hh * dk + half:(hh + 1) * dk]
            out_ref[:, hh * dk:hh * dk + half] = ((x1 * cos - x2 * sin) * scale).astype(BF16)
            out_ref[:, hh * dk + half:(hh + 1) * dk] = ((x1 * sin + x2 * cos) * scale).astype(BF16)

    @pl.when(j < bounds[0])
    def _():
        rotary(q_ref, 1.0)

    @pl.when((j >= bounds[0]) & (j < bounds[1]))
    def _():
        rotary(k_ref, k_scale)

    @pl.when((j >= bounds[1]) & (j < bounds[2]))
    def _():
        v_ref[...] = acc.astype(BF16)

    @pl.when(j >= bounds[2])
    def _():
        sg_ref[...] = (acc * _sigmoid(acc)).astype(BF16)


def _ret_in_proj(x, g, w, layer, cos, sin, dqk, dv_total, dk, tm):
    m, d = x.shape
    tn = _pick(math.gcd(dqk, dv_total), (1024, 512, 256))
    nq, nv = dqk // tn, dv_total // tn
    bounds = (nq, 2 * nq, 2 * nq + nv)

    def sec(start, count):
        return pl.BlockSpec((tm, tn), lambda i, j: (i, jnp.clip(j - start, 0, count - 1)))

    kern = functools.partial(_ret_in_kernel, bounds=bounds, dk=dk, k_scale=dk ** -0.5)
    return pl.pallas_call(
        kern,
        grid=(m // tm, 2 * nq + 2 * nv),
        in_specs=[
            pl.BlockSpec((tm, d), lambda i, j: (i, 0)),
            pl.BlockSpec((1, d), lambda i, j: (0, 0)),
            pl.BlockSpec((None, d, tn), lambda i, j: (layer, 0, j)),
            pl.BlockSpec((tm, dk // 2), lambda i, j: (i, 0)),
            pl.BlockSpec((tm, dk // 2), lambda i, j: (i, 0)),
        ],
        out_specs=[sec(0, nq), sec(nq, nq), sec(2 * nq, nv), sec(2 * nq + nv, nv)],
        out_shape=[
            jax.ShapeDtypeStruct((m, dqk), BF16),
            jax.ShapeDtypeStruct((m, dqk), BF16),
            jax.ShapeDtypeStruct((m, dv_total), BF16),
            jax.ShapeDtypeStruct((m, dv_total), BF16),
        ],
        scratch_shapes=[pltpu.VMEM((tm, d), BF16)],
        compiler_params=_params(("arbitrary", "arbitrary")),
        name="ret_in_proj",
    )(x, g, w, cos, sin)


def _retention_kernel(*refs, valid_rows, has_init, dk, dv):
    if has_init:
        q_ref, k_ref, v_ref, sg_ref, gn_ref, lg_ref, s0_ref, y_ref, s_out_ref, s_ref, decay_ref = refs
    else:
        q_ref, k_ref, v_ref, sg_ref, gn_ref, lg_ref, y_ref, s_out_ref, s_ref, decay_ref = refs
    c = pl.program_id(1)
    nchunks = pl.num_programs(1)
    rows = q_ref.shape[0]
    heads = q_ref.shape[1] // dk

    @pl.when(c == 0)
    def _():
        s_ref[...] = s0_ref[0] if has_init else jnp.zeros_like(s_ref)
        ri = lax.broadcasted_iota(jnp.int32, (rows, rows), 0)
        ci = lax.broadcasted_iota(jnp.int32, (rows, rows), 1)
        diff = (ri - ci).astype(F32)
        for h in range(heads):
            decay_ref[h] = jnp.where(diff >= 0, jnp.exp(jnp.maximum(diff, 0.0) * lg_ref[h][:, :1]), 0.0)

    rowf = lax.broadcasted_iota(jnp.int32, (rows, 1), 0).astype(F32)
    nv = jnp.clip(valid_rows - c * rows, 0, rows).astype(F32)
    valid = rowf < nv
    k_lag = jnp.maximum(nv - 1.0 - rowf, 0.0)

    for h in range(heads):
        qs = slice(h * dk, (h + 1) * dk)
        vs = slice(h * dv, (h + 1) * dv)
        q = q_ref[:, qs]
        k = k_ref[:, qs]
        v = v_ref[:, vs]
        lgam = lg_ref[h][:, :1]
        scores = _dot_nt(q, k) * decay_ref[h]
        state = s_ref[h]
        o = _dot(scores.astype(BF16), v) + _dot(q, state.astype(BF16)) * jnp.exp((rowf + 1.0) * lgam)
        kd = (k.astype(F32) * jnp.where(valid, jnp.exp(k_lag * lgam), 0.0)).astype(BF16)
        new_state = jnp.exp(nv * lgam) * state + _dot_tn(kd, v)
        s_ref[h] = new_state

        @pl.when(c == nchunks - 1)
        def _():
            s_out_ref[0, h] = new_state

        ms = jnp.mean(o * o, axis=-1, keepdims=True)
        on = o * lax.rsqrt(ms + EPS) * gn_ref[h]
        y_ref[:, vs] = (sg_ref[:, vs].astype(F32) * on).astype(BF16)


def _retention(q, k, v, sg, gnorm, log_gamma, s0, layer, nseq, rows, valid_rows, dk, dv):
    m = q.shape[0]
    heads = q.shape[1] // dk
    nchunks = m // (nseq * rows)
    blk = lambda s, c: (s * nchunks + c, 0)
    whole = lambda s, c: (0, 0, 0)
    in_specs = [
        pl.BlockSpec((rows, heads * dk), blk),
        pl.BlockSpec((rows, heads * dk), blk),
        pl.BlockSpec((rows, heads * dv), blk),
        pl.BlockSpec((rows, heads * dv), blk),
        pl.BlockSpec((heads, 1, dv), whole),
        pl.BlockSpec((heads, 1, LANES), whole),
    ]
    args = [q, k, v, sg, gnorm, log_gamma]
    if s0 is not None:
        in_specs.append(pl.BlockSpec((None, 1, heads, dk, dv), lambda s, c: (layer, s, 0, 0, 0)))
        args.append(s0)
    return pl.pallas_call(
        functools.partial(_retention_kernel, valid_rows=valid_rows, has_init=s0 is not None, dk=dk, dv=dv),
        grid=(nseq, nchunks),
        in_specs=in_specs,
        out_specs=[pl.BlockSpec((rows, heads * dv), blk),
                   pl.BlockSpec((1, heads, dk, dv), lambda s, c: (s, 0, 0, 0))],
        out_shape=[jax.ShapeDtypeStruct((m, heads * dv), BF16),
                   jax.ShapeDtypeStruct((nseq, heads, dk, dv), F32)],
        scratch_shapes=[pltpu.VMEM((heads, dk, dv), F32),
                        pltpu.VMEM((heads, rows, rows), F32)],
        compiler_params=_params(("arbitrary", "arbitrary")),
        name="retention_scan",
    )(*args)


def _mlp_kernel(x_ref, g_ref, wu_ref, wd_ref, o_ref, xn_ref):
    j = pl.program_id(1)

    @pl.when(j == 0)
    def _():
        x = x_ref[...]
        xn_ref[...] = _rms_rows(x, g_ref[...]).astype(BF16)
        o_ref[...] = x

    hid = jnp.maximum(_dot(xn_ref[...], wu_ref[...]), 0.0)
    o_ref[...] += _dot((hid * hid).astype(BF16), wd_ref[...])


def _mlp(x, g, w_up, w_down, layer, tm):
    m, d = x.shape
    f = w_up.shape[2]
    tf = _pick(f, (512, 256, 128))
    return pl.pallas_call(
        _mlp_kernel,
        grid=(m // tm, f // tf),
        in_specs=[pl.BlockSpec((tm, d), lambda i, j: (i, 0)),
                  pl.BlockSpec((1, d), lambda i, j: (0, 0)),
                  pl.BlockSpec((None, d, tf), lambda i, j: (layer, 0, j)),
                  pl.BlockSpec((None, tf, d), lambda i, j: (layer, j, 0))],
        out_specs=pl.BlockSpec((tm, d), lambda i, j: (i, 0)),
        out_shape=jax.ShapeDtypeStruct((m, d), F32),
        scratch_shapes=[pltpu.VMEM((tm, d), BF16)],
        compiler_params=_params(("arbitrary", "arbitrary")),
        name="sq_relu_mlp",
    )(x, g, w_up, w_down)


def _final_norm_kernel(x_ref, g_ref, o_ref):
    o_ref[...] = _rms_rows(x_ref[...], g_ref[...])


def _final_norm(x, g, tr):
    m, d = x.shape
    return pl.pallas_call(
        _final_norm_kernel,
        grid=(m // tr,),
        in_specs=[pl.BlockSpec((tr, d), lambda i: (i, 0)), pl.BlockSpec((1, d), lambda i: (0, 0))],
        out_specs=pl.BlockSpec((tr, d), lambda i: (i, 0)),
        out_shape=jax.ShapeDtypeStruct((m, d), F32),
        compiler_params=_params(("arbitrary",)),
        name="final_norm",
    )(x, g)


def _final_norm_shift_kernel(xa_ref, xb_ref, g_ref, o_ref, *, shift):
    tr = o_ref.shape[0]
    o_ref[:tr - shift, :] = _rms_rows(xa_ref[shift:, :], g_ref[...])
    o_ref[tr - shift:, :] = _rms_rows(xb_ref[...], g_ref[...])


def _final_norm_shift(x, g, shift, rows_out, tr):
    d = x.shape[1]
    return pl.pallas_call(
        functools.partial(_final_norm_shift_kernel, shift=shift),
        grid=(rows_out // tr,),
        in_specs=[pl.BlockSpec((tr, d), lambda i: (i, 0)),
                  pl.BlockSpec((shift, d), lambda i: ((i + 1) * (tr // shift), 0)),
                  pl.BlockSpec((1, d), lambda i: (0, 0))],
        out_specs=pl.BlockSpec((tr, d), lambda i: (i, 0)),
        out_shape=jax.ShapeDtypeStruct((rows_out, d), F32),
        compiler_params=_params(("arbitrary",)),
        name="final_norm_prompt",
    )(x, x, g)


def _rotary_tables(pos, half):
    inv = jnp.power(ROPE_BASE, -jnp.arange(half, dtype=F32) / half)
    ang = pos.astype(F32)[:, None] * inv[None, :]
    return jnp.cos(ang), jnp.sin(ang)


def _lane_pad(x, width=LANES):
    return jnp.pad(x, ((0, 0),) * (x.ndim - 1) + ((0, width - x.shape[-1]),))


def kernel(x_prompt, x_sample, cache_fox_k, cache_fox_v, cache_fox_logf, state_ret, meta_tokens,
           norm_mix, norm_mlp, norm_final, fox_w_in, fox_g_q, fox_g_k, fox_b_f, fox_w_o,
           ret_w_in, ret_g_norm, ret_w_o, mlp_w_up, mlp_w_down):
    batch, seq, d = x_prompt.shape
    assert batch == 1, "one prompt stream"
    nstream, t, _ = x_sample.shape
    n_meta = meta_tokens.shape[0]
    depth = norm_mix.shape[0]
    n_fox, _, past, h_fox, dh = cache_fox_k.shape
    n_ret, _, h_ret, dk, dv = state_ret.shape
    dqk, dvt = h_ret * dk, h_ret * dv
    l = n_meta + seq
    lp = -(-l // ROW_ALIGN) * ROW_ALIGN
    ms = nstream * t
    hp = -(-h_fox // 8) * 8
    assert h_fox <= LANES and past % t == 0 and past % LANES == 0
    assert seq % ROW_ALIGN == 0 and n_meta % 8 == 0 and ROW_ALIGN % n_meta == 0

    tm_p = _pick(lp, (640, 512, 256))
    tm_s = _pick(ms, (512, 256, 128, 64))
    tc = ROW_ALIGN
    tq = _pick(lp, (5 * tc, tc))
    ret_rows = _pick(lp, (256,))

    hp_rows = jnp.concatenate(
        [meta_tokens.astype(x_prompt.dtype), x_prompt[0], jnp.zeros((lp - l, d), x_prompt.dtype)], axis=0)
    hs_rows = x_sample.reshape(ms, d)

    cos_p, sin_p = _rotary_tables(jnp.arange(lp), dk // 2)
    cos_s, sin_s = _rotary_tables(n_meta + past + (jnp.arange(ms) % t), dk // 2)
    log_gamma = jnp.log(1.0 - jnp.power(2.0, -5.0 - jnp.arange(h_ret, dtype=F32)))
    log_gamma = jnp.broadcast_to(log_gamma[:, None, None], (h_ret, 1, LANES))

    lc = -(-(past + t) // LANES) * LANES

    outs = {name: [] for name in ("kp", "vp", "fp", "sp", "ks", "vs", "fs", "ss")}
    row = lambda a: a.reshape(1, -1).astype(F32)
    fox_w_in16, fox_w_o16 = fox_w_in.astype(BF16), fox_w_o.astype(BF16)
    ret_w_in16, ret_w_o16 = ret_w_in.astype(BF16), ret_w_o.astype(BF16)
    mlp_w_up16, mlp_w_down16 = mlp_w_up.astype(BF16), mlp_w_down.astype(BF16)
    state_f32 = state_ret.astype(F32)
    for i in range(depth):
        j = i // 2
        g_mix = row(norm_mix[i])
        if i % 2 == 0:
            w_f = _lane_pad(fox_w_in16[j][:, 4 * d:])
            b_f = _lane_pad(row(fox_b_f[j]))
            gq, gk = row(fox_g_q[j]), row(fox_g_k[j])

            q, k32, v32, kb, vb, sg, lf = _fox_in_proj(hp_rows, g_mix, fox_w_in16, j, w_f, gq, gk, b_f, l, tm_p)
            c, ct = _cumsum(lf[None], tc, _pick(lp // tc, (13, 8, 5, 4, 2, 1)), hp)
            logit_bound = 1.02 * math.sqrt(dh) * jnp.max(jnp.abs(gq)) * jnp.max(jnp.abs(gk))
            klo = _kv_skip_plan(c[0], logit_bound, h_fox, tq, tc)
            attend = functools.partial(_fox_prompt_attention, klo, q, kb, vb, c[0], ct[0], sg, dh, tq, tc)
            og = lax.cond(logit_bound <= FAST_MAX_LOGIT,
                          functools.partial(attend, online=False), functools.partial(attend, online=True))
            hp_rows = _out_proj(og, fox_w_o16, j, hp_rows, tm_p)
            outs["kp"].append(k32.reshape(1, l, h_fox, dh))
            outs["vp"].append(v32.reshape(1, l, h_fox, dh))
            outs["fp"].append(lf[:l, :h_fox].reshape(1, l, h_fox))

            q, k32, v32, kb, vb, sg, lf = _fox_in_proj(hs_rows, g_mix, fox_w_in16, j, w_f, gq, gk, b_f, ms, tm_s)
            lf_new = lf.reshape(nstream, t, LANES)
            lf_all = jnp.concatenate(
                [_lane_pad(cache_fox_logf[j].astype(F32)), lf_new,
                 jnp.zeros((nstream, lc - past - t, LANES), F32)], axis=1)
            c, ct = _cumsum(lf_all, LANES, lc // LANES, hp)
            ct = ct.transpose(0, 2, 1, 3).reshape(nstream, 1, hp, lc)
            og = _fox_sample_attention(q, cache_fox_k, cache_fox_v, j, kb, vb, c, ct, sg, t)
            hs_rows = _out_proj(og, fox_w_o16, j, hs_rows, tm_s)
            outs["ks"].append(k32.reshape(nstream, t, h_fox, dh))
            outs["vs"].append(v32.reshape(nstream, t, h_fox, dh))
            outs["fs"].append(lf_new[:, :, :h_fox])
        else:
            gn = ret_g_norm[j].reshape(h_ret, 1, dv).astype(F32)

            q, k, v, sg = _ret_in_proj(hp_rows, g_mix, ret_w_in16, j, cos_p, sin_p, dqk, dvt, dk, tm_p)
            y, s_fin = _retention(q, k, v, sg, gn, log_gamma, None, j, 1, ret_rows, l, dk, dv)
            hp_rows = _out_proj(y, ret_w_o16, j, hp_rows, tm_p)
            outs["sp"].append(s_fin.astype(state_ret.dtype))

            q, k, v, sg = _ret_in_proj(hs_rows, g_mix, ret_w_in16, j, cos_s, sin_s, dqk, dvt, dk, tm_s)
            y, s_new = _retention(q, k, v, sg, gn, log_gamma, state_f32, j, nstream, t, t, dk, dv)
            hs_rows = _out_proj(y, ret_w_o16, j, hs_rows, tm_s)
            outs["ss"].append(s_new.astype(state_ret.dtype))

        g_mlp = row(norm_mlp[i])
        hp_rows = _mlp(hp_rows, g_mlp, mlp_w_up16, mlp_w_down16, i, tm_p)
        hs_rows = _mlp(hs_rows, g_mlp, mlp_w_up16, mlp_w_down16, i, tm_s)

    g_fin = row(norm_final)
    y_prompt = _final_norm_shift(hp_rows, g_fin, n_meta, seq, ROW_ALIGN).reshape(1, seq, d)
    y_sample = _final_norm(hs_rows, g_fin, tm_s).reshape(nstream, t, d)
    stack = lambda name: jnp.stack(outs[name], axis=0)
    return (y_prompt, y_sample, stack("kp"), stack("vp"), stack("fp"), stack("sp"),
            stack("ks"), stack("vs"), stack("fs"), stack("ss"))
```

```python
import functools
import math

import jax
import jax.numpy as jnp
from jax import lax
from jax.experimental import pallas as pl
from jax.experimental.pallas import tpu as pltpu

EPS = 1e-6
NEG_INF = -1e30
ROPE_BASE = 10000.0
LOG2E = math.log2(math.e)
SKIP_LOG = 110.0
FAST_MAX_LOGIT = 60.0
KV_UNROLL = 4

LANES = 128
ROW_ALIGN = 256
VMEM_LIMIT_BYTES = 56 * 1024 * 1024

F32 = jnp.float32
BF16 = jnp.bfloat16


def _pick(n, candidates):
    for c in candidates:
        if c <= n and n % c == 0:
            return c
    return n


def _params(semantics):
    return pltpu.CompilerParams(dimension_semantics=semantics, vmem_limit_bytes=VMEM_LIMIT_BYTES)


def _rms_rows(x, g):
    ms = jnp.mean(x * x, axis=-1, keepdims=True)
    return x * lax.rsqrt(ms + EPS) * g


def _log_sigmoid(x):
    return jnp.minimum(x, 0.0) - jnp.log(1.0 + jnp.exp(-jnp.abs(x)))


def _sigmoid(x):
    return 0.5 * jnp.tanh(0.5 * x) + 0.5


def _dot(a, b):
    return jnp.dot(a, b, preferred_element_type=F32)


def _dot_nt(a, b):
    return lax.dot_general(a, b, (((1,), (1,)), ((), ())), preferred_element_type=F32)


def _dot_tn(a, b):
    return lax.dot_general(a, b, (((0,), (0,)), ((), ())), preferred_element_type=F32)


def _section_index(sec, nsec):
    return lambda i, j: (i, jnp.clip(j - sec * nsec, 0, nsec - 1))


def _fox_in_kernel(x_ref, g_ref, w_ref, wf_ref, gq_ref, gk_ref, bf_ref,
                   q_ref, k32_ref, v32_ref, kb_ref, vb_ref, sg_ref, lf_ref, xn_ref,
                   *, nsec, dh, q_scale):
    j = pl.program_id(1)

    @pl.when(j == 0)
    def _():
        xn = _rms_rows(x_ref[...], g_ref[...]).astype(BF16)
        xn_ref[...] = xn
        lf_ref[...] = _log_sigmoid(_dot(xn, wf_ref[...]) + bf_ref[...])

    acc = _dot(xn_ref[...], w_ref[...])
    sec = j // nsec
    heads = acc.shape[1] // dh

    @pl.when(sec == 0)
    def _():
        for hh in range(heads):
            sl = slice(hh * dh, (hh + 1) * dh)
            q_ref[:, sl] = (_rms_rows(acc[:, sl], gq_ref[...]) * q_scale).astype(BF16)

    @pl.when(sec == 1)
    def _():
        for hh in range(heads):
            sl = slice(hh * dh, (hh + 1) * dh)
            kn = _rms_rows(acc[:, sl], gk_ref[...])
            k32_ref[:, sl] = kn
            kb_ref[:, sl] = kn.astype(BF16)

    @pl.when(sec == 2)
    def _():
        v32_ref[...] = acc
        vb_ref[...] = acc.astype(BF16)

    @pl.when(sec == 3)
    def _():
        sg_ref[...] = _sigmoid(acc).astype(BF16)


def _fox_in_proj(x, g, w, layer, wf, gq, gk, bf, rows_out, tm):
    m, d = x.shape
    dh = gq.shape[-1]
    tn = _pick(d, (1024, 512, 256, 128))
    nsec = d // tn
    kern = functools.partial(_fox_in_kernel, nsec=nsec, dh=dh, q_scale=dh ** -0.5 * LOG2E)
    sec = lambda s: pl.BlockSpec((tm, tn), _section_index(s, nsec))
    const = lambda shape: pl.BlockSpec(shape, lambda i, j: (0, 0))
    return pl.pallas_call(
        kern,
        grid=(m // tm, 4 * nsec),
        in_specs=[
            pl.BlockSpec((tm, d), lambda i, j: (i, 0)),
            const((1, d)),
            pl.BlockSpec((None, d, tn), lambda i, j: (layer, 0, j)),
            const((d, LANES)),
            const((1, dh)),
            const((1, dh)),
            const((1, LANES)),
        ],
        out_specs=[sec(0), sec(1), sec(2), sec(1), sec(2), sec(3),
                   pl.BlockSpec((tm, LANES), lambda i, j: (i, 0))],
        out_shape=[
            jax.ShapeDtypeStruct((m, d), BF16),
            jax.ShapeDtypeStruct((rows_out, d), F32),
            jax.ShapeDtypeStruct((rows_out, d), F32),
            jax.ShapeDtypeStruct((m, d), BF16),
            jax.ShapeDtypeStruct((m, d), BF16),
            jax.ShapeDtypeStruct((m, d), BF16),
            jax.ShapeDtypeStruct((m, LANES), F32),
        ],
        scratch_shapes=[pltpu.VMEM((tm, d), BF16)],
        compiler_params=_params(("arbitrary", "arbitrary")),
        name="fox_in_proj",
    )(x, g, w, wf, gq, gk, bf)


def _cumsum_kernel(lf_ref, c_ref, ct_ref, carry_ref, *, hp, tr, nb):
    @pl.when(pl.program_id(1) == 0)
    def _():
        carry_ref[...] = jnp.zeros_like(carry_ref)

    row = lax.broadcasted_iota(jnp.int32, (tr, tr), 0)
    col = lax.broadcasted_iota(jnp.int32, (tr, tr), 1)
    tri = (col <= row).astype(F32)
    carry = carry_ref[...]
    for r in range(nb):
        rows = slice(r * tr, (r + 1) * tr)
        c = jnp.dot(tri, lf_ref[0, rows, :], preferred_element_type=F32,
                    precision=lax.Precision.HIGHEST) + carry
        c_ref[0, rows, :] = c
        ct_ref[0, r] = c.T[:hp, :]
        carry = c[tr - 1:tr, :]
    carry_ref[...] = carry


def _cumsum(lf, tr, nb, hp):
    nseq, n, _ = lf.shape
    nblk = n // tr
    return pl.pallas_call(
        functools.partial(_cumsum_kernel, hp=hp, tr=tr, nb=nb),
        grid=(nseq, nblk // nb),
        in_specs=[pl.BlockSpec((1, nb * tr, LANES), lambda s, b: (s, b, 0))],
        out_specs=[pl.BlockSpec((1, nb * tr, LANES), lambda s, b: (s, b, 0)),
                   pl.BlockSpec((1, nb, hp, tr), lambda s, b: (s, b, 0, 0))],
        out_shape=[jax.ShapeDtypeStruct((nseq, n, LANES), F32),
                   jax.ShapeDtypeStruct((nseq, nblk, hp, tr), F32)],
        scratch_shapes=[pltpu.VMEM((1, LANES), F32)],
        compiler_params=_params(("arbitrary", "arbitrary")),
        name="logf_cumsum",
    )(lf)


def _head_column(c_blk, h):
    lane = lax.broadcasted_iota(jnp.int32, c_blk.shape, 1)
    return jnp.sum(jnp.where(lane == h, c_blk, 0.0), axis=-1, keepdims=True)


def _fox_prompt_kernel(klo_ref, q_ref, k_ref, v_ref, c_ref, ct_ref, sg_ref, o_ref, vaug_ref, acc_ref, m_ref,
                       *, tk, online):
    h = pl.program_id(0)
    qi = pl.program_id(1)
    nq = pl.num_programs(1)
    tq, dh = q_ref.shape

    @pl.when(qi == 0)
    def _():
        ones_col = (lax.broadcasted_iota(jnp.int32, (tk, dh), 1) == 0).astype(BF16)

        def fill(b, carry):
            start = pl.multiple_of(b * tk, tk)
            vaug_ref[pl.ds(start, tk), :dh] = v_ref[pl.ds(start, tk), :]
            vaug_ref[pl.ds(start, tk), dh:] = ones_col
            return carry

        lax.fori_loop(0, v_ref.shape[0] // tk, fill, 0)

    cq = _head_column(c_ref[...], h) * LOG2E
    blocks_per_q = tq // tk
    acc_ref[...] = jnp.zeros_like(acc_ref)
    if online:
        m_ref[...] = jnp.full_like(m_ref, NEG_INF)

    def visit(kb, chunks, diag_chunk):
        start = pl.multiple_of(kb * tk, tk)
        k = k_ref[pl.ds(start, tk), :]
        v_aug = vaug_ref[pl.ds(start, tk), :]
        ck = ct_ref[kb, pl.ds(h, 1), :] * LOG2E
        for r in chunks:
            rows = slice(r * tk, (r + 1) * tk)
            s = _dot_nt(q_ref[rows, :], k) + (cq[rows] - ck)
            if r == diag_chunk:
                row = lax.broadcasted_iota(jnp.int32, (tk, tk), 0)
                col = lax.broadcasted_iota(jnp.int32, (tk, tk), 1)
                s = jnp.where(col <= row, s, NEG_INF)
            if online:
                m_new = jnp.maximum(m_ref[rows, :], jnp.max(s, axis=-1, keepdims=True))
                acc_ref[rows, :] = (jnp.exp2(m_ref[rows, :] - m_new) * acc_ref[rows, :]
                                    + _dot(jnp.exp2(s - m_new).astype(BF16), v_aug))
                m_ref[rows, :] = m_new
            else:
                acc_ref[rows, :] += _dot(jnp.exp2(s).astype(BF16), v_aug)

    def before_tile(kb, carry):
        visit(kb, range(blocks_per_q), None)
        return carry

    def before_tile_unrolled(t, base):
        for u in range(KV_UNROLL):
            visit(base + t * KV_UNROLL + u, range(blocks_per_q), None)
        return base

    first_diag = qi * blocks_per_q
    lo = klo_ref[h * nq + qi]
    rem = lax.rem(first_diag - lo, KV_UNROLL)
    lax.fori_loop(lo, lo + rem, before_tile, 0)
    lax.fori_loop(0, (first_diag - lo - rem) // KV_UNROLL, before_tile_unrolled, lo + rem)
    for d in range(blocks_per_q):
        visit(first_diag + d, range(d, blocks_per_q), d)
    acc = acc_ref[...]
    o_ref[...] = (acc[:, :dh] / acc[:, dh:dh + 1] * sg_ref[...].astype(F32)).astype(BF16)


def _fox_prompt_attention(klo, q, kb, vb, c, ct, sg, dh, tq, tk, online):
    lp, d = q.shape
    heads = d // dh
    nkb, hp, _ = ct.shape
    col = lambda h, i, klo_ref: (i, h)
    grid_spec = pltpu.PrefetchScalarGridSpec(
        num_scalar_prefetch=1,
        grid=(heads, lp // tq),
        in_specs=[
            pl.BlockSpec((tq, dh), col),
            pl.BlockSpec((lp, dh), lambda h, i, klo_ref: (0, h)),
            pl.BlockSpec((lp, dh), lambda h, i, klo_ref: (0, h)),
            pl.BlockSpec((tq, LANES), lambda h, i, klo_ref: (i, 0)),
            pl.BlockSpec((nkb, hp, tk), lambda h, i, klo_ref: (0, 0, 0)),
            pl.BlockSpec((tq, dh), col),
        ],
        out_specs=pl.BlockSpec((tq, dh), col),
        scratch_shapes=[pltpu.VMEM((lp, 2 * dh), BF16),
                        pltpu.VMEM((tq, 2 * dh), F32),
                        pltpu.VMEM((tq if online else 8, 1), F32)],
    )
    return pl.pallas_call(
        functools.partial(_fox_prompt_kernel, tk=tk, online=online),
        grid_spec=grid_spec,
        out_shape=jax.ShapeDtypeStruct((lp, d), BF16),
        compiler_params=_params(("arbitrary", "arbitrary")),
        name="fox_prompt_attention_online" if online else "fox_prompt_attention",
    )(klo, q, kb, vb, c, ct, sg)


def _kv_skip_plan(c, logit_bound, heads, tq, tk):
    lp = c.shape[0]
    nq, nk = lp // tq, lp // tk
    c_tile = c[::tq, :heads]
    c_blk = c[tk - 1::tk, :heads]
    before = (jnp.arange(nk)[None, :] + 1) * tk <= jnp.arange(nq)[:, None] * tq
    dead = (c_tile[:, None, :] - c_blk[None, :, :] + 2.0 * logit_bound < -SKIP_LOG) & before[:, :, None]
    return jnp.sum(dead, axis=1).astype(jnp.int32).T.reshape(-1)


def _fox_sample_kernel(q_ref, kc_ref, vc_ref, kn_ref, vn_ref, c_ref, ct_ref, sg_ref, o_ref):
    t = q_ref.shape[0]
    past, heads, dh = kc_ref.shape
    row = lax.broadcasted_iota(jnp.int32, (t, t), 0)
    col = lax.broadcasted_iota(jnp.int32, (t, t), 1)
    c_new = c_ref[0] * LOG2E
    ck_all = ct_ref[0, 0] * LOG2E
    for h in range(heads):
        cols = slice(h * dh, (h + 1) * dh)
        q = q_ref[:, cols]
        cq = c_new[:, h:h + 1]
        ck = ck_all[h:h + 1, :]
        s_old = _dot_nt(q, kc_ref[:, h, :].astype(BF16)) + (cq - ck[:, :past])
        s_new = _dot_nt(q, kn_ref[:, cols]) + (cq - ck[:, past:past + t])
        s_new = jnp.where(col <= row, s_new, NEG_INF)
        m = jnp.maximum(jnp.max(s_old, axis=-1, keepdims=True), jnp.max(s_new, axis=-1, keepdims=True))
        p_old = jnp.exp2(s_old - m)
        p_new = jnp.exp2(s_new - m)
        l = jnp.sum(p_old, axis=-1, keepdims=True) + jnp.sum(p_new, axis=-1, keepdims=True)
        acc = (_dot(p_old.astype(BF16), vc_ref[:, h, :].astype(BF16))
               + _dot(p_new.astype(BF16), vn_ref[:, cols]))
        o_ref[:, cols] = (acc / l * sg_ref[:, cols].astype(F32)).astype(BF16)


def _fox_sample_attention(q, k_cache, v_cache, layer, kb, vb, c, ct, sg, t):
    ms, d = q.shape
    _, nstream, past, heads, dh = k_cache.shape
    _, _, hp, lc = ct.shape
    rows = pl.BlockSpec((t, d), lambda b: (b, 0))
    cache = pl.BlockSpec((None, None, past, heads, dh), lambda b: (layer, b, 0, 0, 0))
    return pl.pallas_call(
        _fox_sample_kernel,
        grid=(nstream,),
        in_specs=[
            rows, cache, cache, rows, rows,
            pl.BlockSpec((1, t, LANES), lambda b: (b, past // t, 0)),
            pl.BlockSpec((1, 1, hp, lc), lambda b: (b, 0, 0, 0)),
            rows,
        ],
        out_specs=rows,
        out_shape=jax.ShapeDtypeStruct((ms, d), BF16),
        compiler_params=_params(("arbitrary",)),
        name="fox_sample_attention",
    )(q, k_cache, v_cache, kb, vb, c, ct, sg)


def _out_proj_kernel(x_ref, w_ref, res_ref, o_ref):
    o_ref[...] = res_ref[...] + _dot(x_ref[...], w_ref[...])


def _out_proj(x, w, layer, res, tm):
    m, k = x.shape
    n = w.shape[2]
    tn = _pick(n, (1024, 512, 256, 128))
    return pl.pallas_call(
        _out_proj_kernel,
        grid=(m // tm, n // tn),
        in_specs=[pl.BlockSpec((tm, k), lambda i, j: (i, 0)),
                  pl.BlockSpec((None, k, tn), lambda i, j: (layer, 0, j)),
                  pl.BlockSpec((tm, tn), lambda i, j: (i, j))],
        out_specs=pl.BlockSpec((tm, tn), lambda i, j: (i, j)),
        out_shape=jax.ShapeDtypeStruct((m, n), F32),
        compiler_params=_params(("arbitrary", "arbitrary")),
        name="out_proj_residual",
    )(x, w, res)


def _ret_in_kernel(x_ref, g_ref, w_ref, cos_ref, sin_ref, q_ref, k_ref, v_ref, sg_ref, xn_ref,
                   *, bounds, dk, k_scale):
    j = pl.program_id(1)

    @pl.when(j == 0)
    def _():
        xn_ref[...] = _rms_rows(x_ref[...], g_ref[...]).astype(BF16)

    acc = _dot(xn_ref[...], w_ref[...])
    half = dk // 2
    heads = acc.shape[1] // dk

    def rotary(out_ref, scale):
        cos = cos_ref[...]
        sin = sin_ref[...]
        for hh in range(heads):
            x1 = acc[:, hh * dk:hh * dk + half]
            x2 = acc[:, hh * dk + half:(hh + 1) * dk]
            out_ref[:, hh * dk:hh * dk + half] = ((x1 * cos - x2 * sin) * scale).astype(BF16)
            out_ref[:, hh * dk + half:(hh + 1) * dk] = ((x1 * sin + x2 * cos) * scale).astype(BF16)

    @pl.when(j < bounds[0])
    def _():
        rotary(q_ref, 1.0)

    @pl.when((j >= bounds[0]) & (j < bounds[1]))
    def _():
        rotary(k_ref, k_scale)

    @pl.when((j >= bounds[1]) & (j < bounds[2]))
    def _():
        v_ref[...] = acc.astype(BF16)

    @pl.when(j >= bounds[2])
    def _():
        sg_ref[...] = (acc * _sigmoid(acc)).astype(BF16)


def _ret_in_proj(x, g, w, layer, cos, sin, dqk, dv_total, dk, tm):
    m, d = x.shape
    tn = _pick(math.gcd(dqk, dv_total), (1024, 512, 256))
    nq, nv = dqk // tn, dv_total // tn
    bounds = (nq, 2 * nq, 2 * nq + nv)

    def sec(start, count):
        return pl.BlockSpec((tm, tn), lambda i, j: (i, jnp.clip(j - start, 0, count - 1)))

    kern = functools.partial(_ret_in_kernel, bounds=bounds, dk=dk, k_scale=dk ** -0.5)
    return pl.pallas_call(
        kern,
        grid=(m // tm, 2 * nq + 2 * nv),
        in_specs=[
            pl.BlockSpec((tm, d), lambda i, j: (i, 0)),
            pl.BlockSpec((1, d), lambda i, j: (0, 0)),
            pl.BlockSpec((None, d, tn), lambda i, j: (layer, 0, j)),
            pl.BlockSpec((tm, dk // 2), lambda i, j: (i, 0)),
            pl.BlockSpec((tm, dk // 2), lambda i, j: (i, 0)),
        ],
        out_specs=[sec(0, nq), sec(nq, nq), sec(2 * nq, nv), sec(2 * nq + nv, nv)],
        out_shape=[
            jax.ShapeDtypeStruct((m, dqk), BF16),
            jax.ShapeDtypeStruct((m, dqk), BF16),
            jax.ShapeDtypeStruct((m, dv_total), BF16),
            jax.ShapeDtypeStruct((m, dv_total), BF16),
        ],
        scratch_shapes=[pltpu.VMEM((tm, d), BF16)],
        compiler_params=_params(("arbitrary", "arbitrary")),
        name="ret_in_proj",
    )(x, g, w, cos, sin)


def _retention_kernel(*refs, valid_rows, has_init, dk, dv):
    if has_init:
        q_ref, k_ref, v_ref, sg_ref, gn_ref, lg_ref, s0_ref, y_ref, s_out_ref, s_ref, decay_ref = refs
    else:
        q_ref, k_ref, v_ref, sg_ref, gn_ref, lg_ref, y_ref, s_out_ref, s_ref, decay_ref = refs
    c = pl.program_id(1)
    nchunks = pl.num_programs(1)
    rows = q_ref.shape[0]
    heads = q_ref.shape[1] // dk

    @pl.when(c == 0)
    def _():
        s_ref[...] = s0_ref[0] if has_init else jnp.zeros_like(s_ref)
        ri = lax.broadcasted_iota(jnp.int32, (rows, rows), 0)
        ci = lax.broadcasted_iota(jnp.int32, (rows, rows), 1)
        diff = (ri - ci).astype(F32)
        for h in range(heads):
            decay_ref[h] = jnp.where(diff >= 0, jnp.exp(jnp.maximum(diff, 0.0) * lg_ref[h][:, :1]), 0.0)

    rowf = lax.broadcasted_iota(jnp.int32, (rows, 1), 0).astype(F32)
    nv = jnp.clip(valid_rows - c * rows, 0, rows).astype(F32)
    valid = rowf < nv
    k_lag = jnp.maximum(nv - 1.0 - rowf, 0.0)

    for h in range(heads):
        qs = slice(h * dk, (h + 1) * dk)
        vs = slice(h * dv, (h + 1) * dv)
        q = q_ref[:, qs]
        k = k_ref[:, qs]
        v = v_ref[:, vs]
        lgam = lg_ref[h][:, :1]
        scores = _dot_nt(q, k) * decay_ref[h]
        state = s_ref[h]
        o = _dot(scores.astype(BF16), v) + _dot(q, state.astype(BF16)) * jnp.exp((rowf + 1.0) * lgam)
        kd = (k.astype(F32) * jnp.where(valid, jnp.exp(k_lag * lgam), 0.0)).astype(BF16)
        new_state = jnp.exp(nv * lgam) * state + _dot_tn(kd, v)
        s_ref[h] = new_state

        @pl.when(c == nchunks - 1)
        def _():
            s_out_ref[0, h] = new_state

        ms = jnp.mean(o * o, axis=-1, keepdims=True)
        on = o * lax.rsqrt(ms + EPS) * gn_ref[h]
        y_ref[:, vs] = (sg_ref[:, vs].astype(F32) * on).astype(BF16)


def _retention(q, k, v, sg, gnorm, log_gamma, s0, layer, nseq, rows, valid_rows, dk, dv):
    m = q.shape[0]
    heads = q.shape[1] // dk
    nchunks = m // (nseq * rows)
    blk = lambda s, c: (s * nchunks + c, 0)
    whole = lambda s, c: (0, 0, 0)
    in_specs = [
        pl.BlockSpec((rows, heads * dk), blk),
        pl.BlockSpec((rows, heads * dk), blk),
        pl.BlockSpec((rows, heads * dv), blk),
        pl.BlockSpec((rows, heads * dv), blk),
        pl.BlockSpec((heads, 1, dv), whole),
        pl.BlockSpec((heads, 1, LANES), whole),
    ]
    args = [q, k, v, sg, gnorm, log_gamma]
    if s0 is not None:
        in_specs.append(pl.BlockSpec((None, 1, heads, dk, dv), lambda s, c: (layer, s, 0, 0, 0)))
        args.append(s0)
    return pl.pallas_call(
        functools.partial(_retention_kernel, valid_rows=valid_rows, has_init=s0 is not None, dk=dk, dv=dv),
        grid=(nseq, nchunks),
        in_specs=in_specs,
        out_specs=[pl.BlockSpec((rows, heads * dv), blk),
                   pl.BlockSpec((1, heads, dk, dv), lambda s, c: (s, 0, 0, 0))],
        out_shape=[jax.ShapeDtypeStruct((m, heads * dv), BF16),
                   jax.ShapeDtypeStruct((nseq, heads, dk, dv), F32)],
        scratch_shapes=[pltpu.VMEM((heads, dk, dv), F32),
                        pltpu.VMEM((heads, rows, rows), F32)],
        compiler_params=_params(("arbitrary", "arbitrary")),
        name="retention_scan",
    )(*args)


def _mlp_kernel(x_ref, g_ref, wu_ref, wd_ref, o_ref, xn_ref):
    j = pl.program_id(1)

    @pl.when(j == 0)
    def _():
        x = x_ref[...]
        xn_ref[...] = _rms_rows(x, g_ref[...]).astype(BF16)
        o_ref[...] = x

    hid = jnp.maximum(_dot(xn_ref[...], wu_ref[...]), 0.0)
    o_ref[...] += _dot((hid * hid).astype(BF16), wd_ref[...])


def _mlp(x, g, w_up, w_down, layer, tm):
    m, d = x.shape
    f = w_up.shape[2]
    tf = _pick(f, (1024, 512, 256, 128))
    return pl.pallas_call(
        _mlp_kernel,
        grid=(m // tm, f // tf),
        in_specs=[pl.BlockSpec((tm, d), lambda i, j: (i, 0)),
                  pl.BlockSpec((1, d), lambda i, j: (0, 0)),
                  pl.BlockSpec((None, d, tf), lambda i, j: (layer, 0, j)),
                  pl.BlockSpec((None, tf, d), lambda i, j: (layer, j, 0))],
        out_specs=pl.BlockSpec((tm, d), lambda i, j: (i, 0)),
        out_shape=jax.ShapeDtypeStruct((m, d), F32),
        scratch_shapes=[pltpu.VMEM((tm, d), BF16)],
        compiler_params=_params(("arbitrary", "arbitrary")),
        name="sq_relu_mlp",
    )(x, g, w_up, w_down)


def _final_norm_kernel(x_ref, g_ref, o_ref):
    o_ref[...] = _rms_rows(x_ref[...], g_ref[...])


def _final_norm(x, g, tr):
    m, d = x.shape
    return pl.pallas_call(
        _final_norm_kernel,
        grid=(m // tr,),
        in_specs=[pl.BlockSpec((tr, d), lambda i: (i, 0)), pl.BlockSpec((1, d), lambda i: (0, 0))],
        out_specs=pl.BlockSpec((tr, d), lambda i: (i, 0)),
        out_shape=jax.ShapeDtypeStruct((m, d), F32),
        compiler_params=_params(("arbitrary",)),
        name="final_norm",
    )(x, g)


def _final_norm_shift_kernel(xa_ref, xb_ref, g_ref, o_ref, *, shift):
    tr = o_ref.shape[0]
    o_ref[:tr - shift, :] = _rms_rows(xa_ref[shift:, :], g_ref[...])
    o_ref[tr - shift:, :] = _rms_rows(xb_ref[...], g_ref[...])


def _final_norm_shift(x, g, shift, rows_out, tr):
    d = x.shape[1]
    return pl.pallas_call(
        functools.partial(_final_norm_shift_kernel, shift=shift),
        grid=(rows_out // tr,),
        in_specs=[pl.BlockSpec((tr, d), lambda i: (i, 0)),
                  pl.BlockSpec((shift, d), lambda i: ((i + 1) * (tr // shift), 0)),
                  pl.BlockSpec((1, d), lambda i: (0, 0))],
        out_specs=pl.BlockSpec((tr, d), lambda i: (i, 0)),
        out_shape=jax.ShapeDtypeStruct((rows_out, d), F32),
        compiler_params=_params(("arbitrary",)),
        name="final_norm_prompt",
    )(x, x, g)


def _rotary_tables(pos, half):
    inv = jnp.power(ROPE_BASE, -jnp.arange(half, dtype=F32) / half)
    ang = pos.astype(F32)[:, None] * inv[None, :]
    return jnp.cos(ang), jnp.sin(ang)


def _lane_pad(x, width=LANES):
    return jnp.pad(x, ((0, 0),) * (x.ndim - 1) + ((0, width - x.shape[-1]),))


def kernel(x_prompt, x_sample, cache_fox_k, cache_fox_v, cache_fox_logf, state_ret, meta_tokens,
           norm_mix, norm_mlp, norm_final, fox_w_in, fox_g_q, fox_g_k, fox_b_f, fox_w_o,
           ret_w_in, ret_g_norm, ret_w_o, mlp_w_up, mlp_w_down):
    batch, seq, d = x_prompt.shape
    assert batch == 1, "one prompt stream"
    nstream, t, _ = x_sample.shape
    n_meta = meta_tokens.shape[0]
    depth = norm_mix.shape[0]
    n_fox, _, past, h_fox, dh = cache_fox_k.shape
    n_ret, _, h_ret, dk, dv = state_ret.shape
    dqk, dvt = h_ret * dk, h_ret * dv
    l = n_meta + seq
    lp = -(-l // ROW_ALIGN) * ROW_ALIGN
    ms = nstream * t
    hp = -(-h_fox // 8) * 8
    assert h_fox <= LANES and past % t == 0 and past % LANES == 0
    assert seq % ROW_ALIGN == 0 and n_meta % 8 == 0 and ROW_ALIGN % n_meta == 0

    tm_p = _pick(lp, (640, 512, 256))
    tm_s = _pick(ms, (512, 256, 128, 64))
    tc = ROW_ALIGN
    tq = _pick(lp, (5 * tc, tc))
    ret_rows = _pick(lp, (256,))

    hp_rows = jnp.concatenate(
        [meta_tokens.astype(x_prompt.dtype), x_prompt[0], jnp.zeros((lp - l, d), x_prompt.dtype)], axis=0)
    hs_rows = x_sample.reshape(ms, d)

    cos_p, sin_p = _rotary_tables(jnp.arange(lp), dk // 2)
    cos_s, sin_s = _rotary_tables(n_meta + past + (jnp.arange(ms) % t), dk // 2)
    log_gamma = jnp.log(1.0 - jnp.power(2.0, -5.0 - jnp.arange(h_ret, dtype=F32)))
    log_gamma = jnp.broadcast_to(log_gamma[:, None, None], (h_ret, 1, LANES))

    lc = -(-(past + t) // LANES) * LANES

    outs = {name: [] for name in ("kp", "vp", "fp", "sp", "ks", "vs", "fs", "ss")}
    row = lambda a: a.reshape(1, -1).astype(F32)
    fox_w_in16, fox_w_o16 = fox_w_in.astype(BF16), fox_w_o.astype(BF16)
    ret_w_in16, ret_w_o16 = ret_w_in.astype(BF16), ret_w_o.astype(BF16)
    mlp_w_up16, mlp_w_down16 = mlp_w_up.astype(BF16), mlp_w_down.astype(BF16)
    state_f32 = state_ret.astype(F32)
    for i in range(depth):
        j = i // 2
        g_mix = row(norm_mix[i])
        if i % 2 == 0:
            w_f = _lane_pad(fox_w_in16[j][:, 4 * d:])
            b_f = _lane_pad(row(fox_b_f[j]))
            gq, gk = row(fox_g_q[j]), row(fox_g_k[j])

            q, k32, v32, kb, vb, sg, lf = _fox_in_proj(hp_rows, g_mix, fox_w_in16, j, w_f, gq, gk, b_f, l, tm_p)
            c, ct = _cumsum(lf[None], tc, _pick(lp // tc, (13, 8, 5, 4, 2, 1)), hp)
            logit_bound = 1.02 * math.sqrt(dh) * jnp.max(jnp.abs(gq)) * jnp.max(jnp.abs(gk))
            klo = _kv_skip_plan(c[0], logit_bound, h_fox, tq, tc)
            attend = functools.partial(_fox_prompt_attention, klo, q, kb, vb, c[0], ct[0], sg, dh, tq, tc)
            og = lax.cond(logit_bound <= FAST_MAX_LOGIT,
                          functools.partial(attend, online=False), functools.partial(attend, online=True))
            hp_rows = _out_proj(og, fox_w_o16, j, hp_rows, tm_p)
            outs["kp"].append(k32.reshape(1, l, h_fox, dh))
            outs["vp"].append(v32.reshape(1, l, h_fox, dh))
            outs["fp"].append(lf[:l, :h_fox].reshape(1, l, h_fox))

            q, k32, v32, kb, vb, sg, lf = _fox_in_proj(hs_rows, g_mix, fox_w_in16, j, w_f, gq, gk, b_f, ms, tm_s)
            lf_new = lf.reshape(nstream, t, LANES)
            lf_all = jnp.concatenate(
                [_lane_pad(cache_fox_logf[j].astype(F32)), lf_new,
                 jnp.zeros((nstream, lc - past - t, LANES), F32)], axis=1)
            c, ct = _cumsum(lf_all, LANES, lc // LANES, hp)
            ct = ct.transpose(0, 2, 1, 3).reshape(nstream, 1, hp, lc)
            og = _fox_sample_attention(q, cache_fox_k, cache_fox_v, j, kb, vb, c, ct, sg, t)
            hs_rows = _out_proj(og, fox_w_o16, j, hs_rows, tm_s)
            outs["ks"].append(k32.reshape(nstream, t, h_fox, dh))
            outs["vs"].append(v32.reshape(nstream, t, h_fox, dh))
            outs["fs"].append(lf_new[:, :, :h_fox])
        else:
            gn = ret_g_norm[j].reshape(h_ret, 1, dv).astype(F32)

            q, k, v, sg = _ret_in_proj(hp_rows, g_mix, ret_w_in16, j, cos_p, sin_p, dqk, dvt, dk, tm_p)
            y, s_fin = _retention(q, k, v, sg, gn, log_gamma, None, j, 1, ret_rows, l, dk, dv)
            hp_rows = _out_proj(y, ret_w_o16, j, hp_rows, tm_p)
            outs["sp"].append(s_fin.astype(state_ret.dtype))

            q, k, v, sg = _ret_in_proj(hs_rows, g_mix, ret_w_in16, j, cos_s, sin_s, dqk, dvt, dk, tm_s)
            y, s_new = _retention(q, k, v, sg, gn, log_gamma, state_f32, j, nstream, t, t, dk, dv)
            hs_rows = _out_proj(y, ret_w_o16, j, hs_rows, tm_s)
            outs["ss"].append(s_new.astype(state_ret.dtype))

        g_mlp = row(norm_mlp[i])
        hp_rows = _mlp(hp_rows, g_mlp, mlp_w_up16, mlp_w_down16, i, tm_p)
        hs_rows = _mlp(hs_rows, g_mlp, mlp_w_up16, mlp_w_down16, i, tm_s)

    g_fin = row(norm_final)
    y_prompt = _final_norm_shift(hp_rows, g_fin, n_meta, seq, ROW_ALIGN).reshape(1, seq, d)
    y_sample = _final_norm(hs_rows, g_fin, tm_s).reshape(nstream, t, d)
    stack = lambda name: jnp.stack(outs[name], axis=0)
    return (y_prompt, y_sample, stack("kp"), stack("vp"), stack("fp"), stack("sp"),
            stack("ks"), stack("vs"), stack("fs"), stack("ss"))
```
